```python
import functools
import jax, jax.numpy as jnp
from jax import lax
import numpy as np


D_MODEL = 1024
BATCH = 4
SEQ = 8192
DEPTH = 2
DEC_BATCH = 128
DEC_SEQ = 8
PAST_LEN = 16384
PAGE_SIZE = 128

HEAD_DIM = 64
N_HEADS = 8
N_KV_HEADS = 2
GROUP = N_HEADS // N_KV_HEADS
WINDOW = 128
BLOCK = 128
ROPE_THETA = 10000.0
Q_W = N_HEADS * HEAD_DIM
KV_W = N_KV_HEADS * HEAD_DIM
MIX_W = Q_W
CONF_W = MIX_W
CONF_K = 31
SC_W = MIX_W
SC_K = 3
N_BRANCH = 3
D_FF = -(-8 * D_MODEL // (3 * 256)) * 256
EPS = 1e-6
_SIZES = (Q_W, KV_W, KV_W, CONF_W, CONF_W, SC_W, SC_W, SC_W, N_BRANCH * D_MODEL)
IN_W = int(sum(_SIZES))
SPLIT_AT = tuple(int(v) for v in np.cumsum(_SIZES)[:-1])

kernel_name = 'hybrid_conformer_shortconv_swa_decoder_step'


def _rmsnorm(x, g):
    x32 = x.astype(jnp.float32)
    y = x32 * lax.rsqrt(jnp.mean(x32 * x32, axis=-1, keepdims=True) + EPS)
    return (y * g.astype(jnp.float32)).astype(x.dtype)


def _layernorm(x, g, b):
    x32 = x.astype(jnp.float32)
    mu = jnp.mean(x32, axis=-1, keepdims=True)
    xc = x32 - mu
    var = jnp.mean(xc * xc, axis=-1, keepdims=True)
    return (xc * lax.rsqrt(var + EPS) * g.astype(jnp.float32) + b.astype(jnp.float32)).astype(x.dtype)


def _rope(x, pos):
    half = HEAD_DIM // 2
    inv = ROPE_THETA ** (-jnp.arange(half, dtype=jnp.float32) / half)
    ang = pos.astype(jnp.float32)[:, None] * inv[None, :]
    cos = jnp.cos(ang)[None, :, None, :]
    sin = jnp.sin(ang)[None, :, None, :]
    x32 = x.astype(jnp.float32)
    x1, x2 = x32[..., :half], x32[..., half:]
    return jnp.concatenate([x1 * cos - x2 * sin, x2 * cos + x1 * sin], axis=-1).astype(x.dtype)


def _causal_dwconv(u, prev, w):
    ext = jnp.concatenate([prev.astype(u.dtype), u], axis=1)
    out = lax.conv_general_dilated(
        ext, w[:, None, :].astype(u.dtype), window_strides=(1,), padding='VALID',
        dimension_numbers=('NWC', 'WIO', 'NWC'), feature_group_count=u.shape[-1])
    return out, ext[:, ext.shape[1] - (w.shape[0] - 1):]


def _sink_softmax(s, mask, sinks):
    sk = sinks.astype(jnp.float32).reshape(N_KV_HEADS, GROUP)[:, :, None, None]
    s = jnp.where(mask, s, -jnp.inf)
    m = jnp.maximum(jnp.max(s, axis=-1, keepdims=True), sk)
    p = jnp.exp(s - m)
    return p / (jnp.sum(p, axis=-1, keepdims=True) + jnp.exp(sk - m))


def _attend_prompt(q, k, v, sinks):
    b, s = q.shape[0], q.shape[1]
    nb = s // BLOCK
    qb = q.reshape(b, nb, BLOCK, N_KV_HEADS, GROUP, HEAD_DIM)
    pad = jnp.zeros((b, BLOCK, N_KV_HEADS, HEAD_DIM), k.dtype)
    kp = jnp.concatenate([pad, k], axis=1).reshape(b, nb + 1, BLOCK, N_KV_HEADS, HEAD_DIM)
    vp = jnp.concatenate([pad, v], axis=1).reshape(b, nb + 1, BLOCK, N_KV_HEADS, HEAD_DIM)
    kb = jnp.concatenate([kp[:, :-1], kp[:, 1:]], axis=2)
    vb = jnp.concatenate([vp[:, :-1], vp[:, 1:]], axis=2)
    sc = jnp.einsum('bnqkgd,bnjkd->bnkgqj', qb, kb,
                    preferred_element_type=jnp.float32) * (HEAD_DIM ** -0.5)
    i = jnp.arange(BLOCK)[:, None]
    j = jnp.arange(2 * BLOCK)[None, :]
    diff = i + BLOCK - j
    key_pos = jnp.arange(nb)[:, None, None] * BLOCK + j[None] - BLOCK
    mask = (diff >= 0)[None] & (diff < WINDOW)[None] & (key_pos >= 0)
    p = _sink_softmax(sc, mask[None, :, None, None], sinks)
    o = jnp.einsum('bnkgqj,bnjkd->bnqkgd', p.astype(v.dtype), vb)
    L = min(WINDOW, s)
    return o.reshape(b, s, Q_W), k[:, s - L:], v[:, s - L:]


def _attend_sample(q, k, v, sinks, k_buf, v_buf):
    b, t = q.shape[0], q.shape[1]
    L = k_buf.shape[1]
    k_all = jnp.concatenate([k_buf.astype(k.dtype), k], axis=1)
    v_all = jnp.concatenate([v_buf.astype(v.dtype), v], axis=1)
    qg = q.reshape(b, t, N_KV_HEADS, GROUP, HEAD_DIM)
    sc = jnp.einsum('btkgd,bjkd->bkgtj', qg, k_all,
                    preferred_element_type=jnp.float32) * (HEAD_DIM ** -0.5)
    q_pos = PAST_LEN + jnp.arange(t)
    k_pos = jnp.concatenate([PAST_LEN - L + jnp.arange(L), q_pos])
    diff = q_pos[:, None] - k_pos[None, :]
    mask = (diff >= 0) & (diff < WINDOW)
    p = _sink_softmax(sc, mask, sinks)
    o = jnp.einsum('bkgtj,bjkd->btkgd', p.astype(v.dtype), v_all)
    n = k_all.shape[1]
    return o.reshape(b, t, Q_W), k_all[:, n - L:], v_all[:, n - L:]


def _layer(x, p, conf_prev, sc_prev, pos, attend):
    (norm1, w_in, sinks, conf_dw_w, conf_dw_b, conf_ln_g, conf_ln_b, sconv_w,
     w_branch, w_out, norm2, w_ffn_in, w_ffn_out) = p
    b, s, _ = x.shape
    h = _rmsnorm(x, norm1)
    z = h @ w_in.astype(h.dtype)
    q, k, v, ca, cg, sb, sc, sx, gl = jnp.split(z, SPLIT_AT, axis=-1)
    q = _rope(q.reshape(b, s, N_HEADS, HEAD_DIM), pos)
    k = _rope(k.reshape(b, s, N_KV_HEADS, HEAD_DIM), pos)
    v = v.reshape(b, s, N_KV_HEADS, HEAD_DIM)
    attn_o, k_state, v_state = attend(q, k, v, sinks)
    u = ca * jax.nn.sigmoid(cg)
    dw, conf_state = _causal_dwconv(u, conf_prev, conf_dw_w)
    conf_o = jax.nn.silu(_layernorm(dw + conf_dw_b.astype(dw.dtype), conf_ln_g, conf_ln_b))
    cu = sc * sx
    conv3, sc_state = _causal_dwconv(cu, sc_prev, sconv_w)
    sc_o = sb * conv3
    gates = jax.nn.sigmoid(gl.reshape(b, s, N_BRANCH, D_MODEL))
    wb = w_branch.astype(h.dtype)
    merged = (gates[:, :, 0] * (attn_o @ wb[0])
              + gates[:, :, 1] * (conf_o @ wb[1])
              + gates[:, :, 2] * (sc_o @ wb[2]))
    x = x + merged @ w_out.astype(h.dtype)
    hn = _rmsnorm(x, norm2)
    gu = hn @ w_ffn_in.astype(hn.dtype)
    g, up = gu[..., :D_FF], gu[..., D_FF:]
    x = x + (jax.nn.silu(g) * up) @ w_ffn_out.astype(hn.dtype)
    return x, k_state, v_state, conf_state, sc_state


def setup_inputs(seed: int = 0) -> dict:
    key = jax.random.key(seed)
    ks = jax.random.split(key, 24)
    f32 = jnp.float32
    L = min(WINDOW, PAST_LEN)
    nrm = lambda k, shape, scale: jax.random.normal(k, shape, f32) * scale
    return {
        'x_prompt': nrm(ks[0], (BATCH, SEQ, D_MODEL), 1.0),
        'x_sample': nrm(ks[1], (DEC_BATCH, DEC_SEQ, D_MODEL), 1.0),
        'cache_k': nrm(ks[2], (DEPTH, DEC_BATCH, L, N_KV_HEADS, HEAD_DIM), 1.0),
        'cache_v': nrm(ks[3], (DEPTH, DEC_BATCH, L, N_KV_HEADS, HEAD_DIM), 1.0),
        'state_conf': nrm(ks[4], (DEPTH, DEC_BATCH, CONF_K - 1, CONF_W), 0.5),
        'state_sconv': nrm(ks[5], (DEPTH, DEC_BATCH, SC_K - 1, SC_W), 0.5),
        'norm1': 1.0 + nrm(ks[6], (DEPTH, D_MODEL), 0.02),
        'w_in': nrm(ks[7], (DEPTH, D_MODEL, IN_W), D_MODEL ** -0.5),
        'sinks': nrm(ks[8], (DEPTH, N_HEADS), 0.5),
        'conf_dw_w': nrm(ks[9], (DEPTH, CONF_K, CONF_W), CONF_K ** -0.5),
        'conf_dw_b': nrm(ks[10], (DEPTH, CONF_W), 0.02),
        'conf_ln_g': 1.0 + nrm(ks[11], (DEPTH, CONF_W), 0.02),
        'conf_ln_b': nrm(ks[12], (DEPTH, CONF_W), 0.02),
        'sconv_w': nrm(ks[13], (DEPTH, SC_K, SC_W), SC_K ** -0.5),
        'w_branch': nrm(ks[14], (DEPTH, N_BRANCH, MIX_W, D_MODEL), MIX_W ** -0.5),
        'w_out': nrm(ks[15], (DEPTH, D_MODEL, D_MODEL), D_MODEL ** -0.5),
        'norm2': 1.0 + nrm(ks[16], (DEPTH, D_MODEL), 0.02),
        'w_ffn_in': nrm(ks[17], (DEPTH, D_MODEL, 2 * D_FF), D_MODEL ** -0.5),
        'w_ffn_out': nrm(ks[18], (DEPTH, D_FF, D_MODEL), D_FF ** -0.5),
        'final_norm': 1.0 + nrm(ks[19], (D_MODEL,), 0.02),
    }


def reference(x_prompt, x_sample, cache_k, cache_v, state_conf, state_sconv,
              norm1, w_in, sinks, conf_dw_w, conf_dw_b, conf_ln_g, conf_ln_b, sconv_w,
              w_branch, w_out, norm2, w_ffn_in, w_ffn_out, final_norm):
    xp, xs = x_prompt, x_sample
    bp, sp = xp.shape[0], xp.shape[1]
    pos_p = jnp.arange(sp)
    pos_s = PAST_LEN + jnp.arange(xs.shape[1])
    kp_l, vp_l, cp_l, sp_l = [], [], [], []
    ks_l, vs_l, cs_l, ss_l = [], [], [], []
    for l in range(DEPTH):
        p = (norm1[l], w_in[l], sinks[l], conf_dw_w[l], conf_dw_b[l], conf_ln_g[l],
             conf_ln_b[l], sconv_w[l], w_branch[l], w_out[l], norm2[l],
             w_ffn_in[l], w_ffn_out[l])
        conf0 = jnp.zeros((bp, CONF_K - 1, CONF_W), xp.dtype)
        sc0 = jnp.zeros((bp, SC_K - 1, SC_W), xp.dtype)
        xp, kp, vp, cp, scp = _layer(xp, p, conf0, sc0, pos_p, _attend_prompt)
        attend_s = functools.partial(_attend_sample, k_buf=cache_k[l], v_buf=cache_v[l])
        xs, kss, vss, css, scs = _layer(xs, p, state_conf[l], state_sconv[l], pos_s, attend_s)
        kp_l.append(kp); vp_l.append(vp); cp_l.append(cp); sp_l.append(scp)
        ks_l.append(kss); vs_l.append(vss); cs_l.append(css); ss_l.append(scs)
    y_prompt = _rmsnorm(xp, final_norm)
    y_sample = _rmsnorm(xs, final_norm)
    return (y_prompt, y_sample,
            jnp.stack(kp_l), jnp.stack(vp_l), jnp.stack(cp_l), jnp.stack(sp_l),
            jnp.stack(ks_l), jnp.stack(vs_l), jnp.stack(cs_l), jnp.stack(ss_l))
```

```python
import functools

import numpy as np
import jax
import jax.numpy as jnp
from jax import lax
from jax.experimental import pallas as pl
from jax.experimental.pallas import tpu as pltpu

D_MODEL = 1024
HEAD_DIM = 64
N_HEADS = 8
N_KV_HEADS = 2
GROUP = N_HEADS // N_KV_HEADS
WINDOW = 128
BLOCK = 128
ROPE_THETA = 10000.0
Q_W = N_HEADS * HEAD_DIM
KV_W = N_KV_HEADS * HEAD_DIM
MIX_W = Q_W
CONF_K = 31
SC_K = 3
N_BRANCH = 3
D_FF = 2816
EPS = 1e-6
PAST_LEN = 16384

_O_Q = 0
_O_K = _O_Q + Q_W
_O_V = _O_K + KV_W
_O_CA = _O_V + KV_W
_O_CG = _O_CA + MIX_W
_O_SB = _O_CG + MIX_W
_O_SC = _O_SB + MIX_W
_O_SX = _O_SC + MIX_W
_O_GL = _O_SX + MIX_W
IN_W = _O_GL + N_BRANCH * D_MODEL

SUBLANES = 8
LANES = 128
VMEM_LIMIT_BYTES = 56 * 1024 * 1024

PROMPT_TILE = 256
SAMPLE_BATCH_TILE = 32
FFN_TILE = 256
FFN_CHUNK = 256
CONF_HIST = 32
SC_HIST = 8

_BF16 = jnp.bfloat16
_F32 = jnp.float32


def _dot(a, b):
    return jnp.dot(a, b, preferred_element_type=_F32)


def _rmsnorm(x, g):
    return x * lax.rsqrt(jnp.mean(x * x, axis=-1, keepdims=True) + EPS) * g


def _rope(x, cos, sin_signed):
    lane = lax.broadcasted_iota(jnp.int32, (x.shape[0], LANES), 1)
    first_half = (lane % HEAD_DIM) < (HEAD_DIM // 2)
    cols = []
    for c in range(x.shape[1] // LANES):
        xb = x[:, c * LANES:(c + 1) * LANES]
        partner = jnp.where(first_half,
                            pltpu.roll(xb, LANES - HEAD_DIM // 2, 1),
                            pltpu.roll(xb, HEAD_DIM // 2, 1))
        cols.append(xb * cos + partner * sin_signed)
    return cols[0] if len(cols) == 1 else jnp.concatenate(cols, axis=-1)


def _sink_attention(q, k, v, mask, sink_col):
    s = lax.dot_general(q, k, (((1,), (1,)), ((), ())), preferred_element_type=_F32)
    s = jnp.where(mask, s, -jnp.inf)
    m = jnp.maximum(jnp.max(s, axis=-1, keepdims=True), sink_col)
    p = jnp.exp(s - m)
    denom = jnp.sum(p, axis=-1, keepdims=True) + jnp.exp(sink_col - m)
    return _dot(p.astype(_BF16), v) / denom


def _layernorm(x, g, b):
    mu = jnp.mean(x, axis=-1, keepdims=True)
    xc = x - mu
    var = jnp.mean(xc * xc, axis=-1, keepdims=True)
    return xc * lax.rsqrt(var + EPS) * g + b


def _conf_conv_rows(ext_ref, w_ref, rows):
    base = CONF_HIST - (CONF_K - 1)
    out = None
    for r in range(SUBLANES):
        g = None
        for a in range((base + CONF_K - 1) // SUBLANES + 1):
            j = a * SUBLANES + r - base
            if j < 0 or j >= CONF_K:
                continue
            term = w_ref[j:j + 1, :] * ext_ref[a * SUBLANES:a * SUBLANES + rows + SUBLANES, :]
            g = term if g is None else g + term
        if g is None:
            continue
        part = g[r:r + rows, :]
        out = part if out is None else out + part
    return out


def _branch_merge(h, win_ref, wb_ref, wout_ref, attn_o, conf_o, sc_o):
    merged = None
    for i, br in enumerate((attn_o, conf_o, sc_o)):
        gate = jax.nn.sigmoid(_dot(h, win_ref[:, _O_GL + i * D_MODEL:_O_GL + (i + 1) * D_MODEL]))
        term = gate * _dot(br.astype(_BF16), wb_ref[i])
        merged = term if merged is None else merged + term
    return _dot(merged.astype(_BF16), wout_ref[...])


def _prompt_mixer_kernel(sinks_ref, x_ref, cos_ref, sin_ref, n1_ref, win_ref, dww_ref, dwb_ref,
                         lng_ref, lnb_ref, scw_ref, wb_ref, wout_ref,
                         xo_ref, kst_ref, vst_ref, cst_ref, sst_ref,
                         kprev_ref, vprev_ref, cext_ref, sext_ref):
    t = PROMPT_TILE
    j = pl.program_id(1)

    @pl.when(j == 0)
    def _():
        kprev_ref[...] = jnp.zeros_like(kprev_ref)
        vprev_ref[...] = jnp.zeros_like(vprev_ref)
        cext_ref[...] = jnp.zeros_like(cext_ref)
        sext_ref[...] = jnp.zeros_like(sext_ref)

    x = x_ref[0]
    h = _rmsnorm(x, n1_ref[...]).astype(_BF16)

    cos = cos_ref[...]
    sin = sin_ref[...]
    q = _rope(_dot(h, win_ref[:, _O_Q:_O_K]), cos, sin) * (HEAD_DIM ** -0.5)
    k = _rope(_dot(h, win_ref[:, _O_K:_O_V]), cos, sin)
    v = _dot(h, win_ref[:, _O_V:_O_CA])
    kst_ref[0] = k[t - WINDOW:, :]
    vst_ref[0] = v[t - WINDOW:, :]
    qb = q.astype(_BF16)
    kb = jnp.concatenate([kprev_ref[...], k.astype(_BF16)], axis=0)
    vb = jnp.concatenate([vprev_ref[...], v.astype(_BF16)], axis=0)
    kprev_ref[...] = kb[t:, :]
    vprev_ref[...] = vb[t:, :]

    qi = lax.broadcasted_iota(jnp.int32, (BLOCK, 2 * BLOCK), 0)
    kj = lax.broadcasted_iota(jnp.int32, (BLOCK, 2 * BLOCK), 1)
    diff = qi + BLOCK - kj
    band = (diff >= 0) & (diff < WINDOW)
    band4 = jnp.concatenate([band] * GROUP, axis=0)
    first_key = jnp.where(j == 0, BLOCK, 0)
    first_mask = band4 & jnp.concatenate([kj >= first_key] * GROUP, axis=0)

    attn_rows = []
    for blk in range(t // BLOCK):
        r0 = blk * BLOCK
        mask = first_mask if blk == 0 else band4
        heads = [None] * N_HEADS
        for kh in range(N_KV_HEADS):
            qs = jnp.concatenate(
                [qb[r0:r0 + BLOCK, (kh * GROUP + g) * HEAD_DIM:(kh * GROUP + g + 1) * HEAD_DIM]
                 for g in range(GROUP)], axis=0)
            sink_col = jnp.concatenate(
                [jnp.full((BLOCK, 1), sinks_ref[kh * GROUP + g], _F32) for g in range(GROUP)], axis=0)
            o = _sink_attention(qs, kb[r0:r0 + 2 * BLOCK, kh * HEAD_DIM:(kh + 1) * HEAD_DIM],
                                vb[r0:r0 + 2 * BLOCK, kh * HEAD_DIM:(kh + 1) * HEAD_DIM],
                                mask, sink_col)
            for g in range(GROUP):
                heads[kh * GROUP + g] = o[g * BLOCK:(g + 1) * BLOCK, :]
        attn_rows.append(jnp.concatenate(heads, axis=-1))
    attn_o = attn_rows[0] if len(attn_rows) == 1 else jnp.concatenate(attn_rows, axis=0)

    ca = _dot(h, win_ref[:, _O_CA:_O_CG])
    cg = _dot(h, win_ref[:, _O_CG:_O_SB])
    u = ca * jax.nn.sigmoid(cg)
    cext_ref[CONF_HIST:CONF_HIST + t, :] = u
    dw = _conf_conv_rows(cext_ref, dww_ref, t)
    cst_ref[0] = u[t - CONF_HIST:, :]
    cext_ref[0:CONF_HIST, :] = u[t - CONF_HIST:, :]
    conf_o = jax.nn.silu(_layernorm(dw + dwb_ref[...], lng_ref[...], lnb_ref[...]))

    sb = _dot(h, win_ref[:, _O_SB:_O_SC])
    cu = _dot(h, win_ref[:, _O_SC:_O_SX]) * _dot(h, win_ref[:, _O_SX:_O_GL])
    sext_ref[SC_HIST:SC_HIST + t, :] = cu
    ext = sext_ref[...]
    conv3 = None
    for jj in range(SC_K):
        off = SC_HIST - (SC_K - 1) + jj
        term = scw_ref[jj:jj + 1, :] * ext[off:off + t, :]
        conv3 = term if conv3 is None else conv3 + term
    sst_ref[0] = cu[t - SC_HIST:, :]
    sext_ref[0:SC_HIST, :] = cu[t - SC_HIST:, :]
    sc_o = sb * conv3

    xo_ref[0] = x + _branch_merge(h, win_ref, wb_ref, wout_ref, attn_o, conf_o, sc_o)


def _sample_mixer_kernel(sinks_ref, x_ref, cos_ref, sin_ref, ck_ref, cv_ref, sconf_ref, ssc_ref,
                         n1_ref, win_ref, dww_ref, dwb_ref, lng_ref, lnb_ref, scw_ref, wb_ref, wout_ref,
                         xo_ref, ko_ref, vo_ref, co_ref, so_ref,
                         q_s, k_s, v_s, u_s, cu_s, attn_s, dw_s, c3_s, cext_s, sext_s):
    bc = SAMPLE_BATCH_TILE
    ts = x_ref.shape[0] // bc
    x = x_ref[...]
    h = _rmsnorm(x, n1_ref[...]).astype(_BF16)
    cos = cos_ref[...]
    sin = sin_ref[...]
    q_s[...] = _rope(_dot(h, win_ref[:, _O_Q:_O_K]), cos, sin) * (HEAD_DIM ** -0.5)
    k_s[...] = _rope(_dot(h, win_ref[:, _O_K:_O_V]), cos, sin)
    v_s[...] = _dot(h, win_ref[:, _O_V:_O_CA])
    u_s[...] = _dot(h, win_ref[:, _O_CA:_O_CG]) * jax.nn.sigmoid(_dot(h, win_ref[:, _O_CG:_O_SB]))
    cu_s[...] = _dot(h, win_ref[:, _O_SC:_O_SX]) * _dot(h, win_ref[:, _O_SX:_O_GL])

    cache_len = ck_ref.shape[1]
    n_keys = cache_len + ts
    tok = lax.broadcasted_iota(jnp.int32, (GROUP * ts, n_keys), 0) % ts
    key = lax.broadcasted_iota(jnp.int32, (GROUP * ts, n_keys), 1)
    diff = jnp.where(key < cache_len, tok + cache_len - key, tok - (key - cache_len))
    mask = (diff >= 0) & (diff < WINDOW)

    def body(b, carry):
        r0 = pl.multiple_of(b * ts, ts)
        rows = pl.ds(r0, ts)
        k_new = k_s[rows, :]
        v_new = v_s[rows, :]
        k_all = jnp.concatenate([ck_ref[b], k_new], axis=0)
        v_all = jnp.concatenate([cv_ref[b], v_new], axis=0)
        ko_ref[b] = k_all[ts:, :]
        vo_ref[b] = v_all[ts:, :]
        kb = k_all.astype(_BF16)
        vb = v_all.astype(_BF16)
        qrow = q_s[rows, :].astype(_BF16)
        heads = [None] * N_HEADS
        for kh in range(N_KV_HEADS):
            qs = jnp.concatenate(
                [qrow[:, (kh * GROUP + g) * HEAD_DIM:(kh * GROUP + g + 1) * HEAD_DIM]
                 for g in range(GROUP)], axis=0)
            sink_col = jnp.concatenate(
                [jnp.full((ts, 1), sinks_ref[kh * GROUP + g], _F32) for g in range(GROUP)], axis=0)
            o = _sink_attention(qs, kb[:, kh * HEAD_DIM:(kh + 1) * HEAD_DIM],
                                vb[:, kh * HEAD_DIM:(kh + 1) * HEAD_DIM], mask, sink_col)
            for g in range(GROUP):
                heads[kh * GROUP + g] = o[g * ts:(g + 1) * ts, :]
        attn_s[rows, :] = jnp.concatenate(heads, axis=-1)

        cext_s[0:CONF_K - 1, :] = sconf_ref[b]
        cext_s[CONF_K - 1:CONF_K - 1 + ts, :] = u_s[rows, :]
        dw = None
        for jj in range(CONF_K):
            term = dww_ref[jj:jj + 1, :] * cext_s[jj:jj + ts, :]
            dw = term if dw is None else dw + term
        dw_s[rows, :] = dw
        co_ref[b] = cext_s[ts:ts + CONF_K - 1, :]

        sext_s[0:SC_K - 1, :] = ssc_ref[b]
        sext_s[SC_K - 1:SC_K - 1 + ts, :] = cu_s[rows, :]
        c3 = None
        for jj in range(SC_K):
            term = scw_ref[jj:jj + 1, :] * sext_s[jj:jj + ts, :]
            c3 = term if c3 is None else c3 + term
        c3_s[rows, :] = c3
        so_ref[b] = sext_s[ts:ts + SC_K - 1, :]
        return carry

    lax.fori_loop(0, bc, body, 0)

    conf_o = jax.nn.silu(_layernorm(dw_s[...] + dwb_ref[...], lng_ref[...], lnb_ref[...]))
    sc_o = _dot(h, win_ref[:, _O_SB:_O_SC]) * c3_s[...]
    xo_ref[...] = x + _branch_merge(h, win_ref, wb_ref, wout_ref, attn_s[...], conf_o, sc_o)


def _ffn_kernel(x_ref, n2_ref, w1_ref, w2_ref, fn_ref, o_ref, *, final):
    x = x_ref[...]
    hn = _rmsnorm(x, n2_ref[...]).astype(_BF16)
    acc = x
    for c in range(D_FF // FFN_CHUNK):
        lo = c * FFN_CHUNK
        g = _dot(hn, w1_ref[:, lo:lo + FFN_CHUNK])
        up = _dot(hn, w1_ref[:, D_FF + lo:D_FF + lo + FFN_CHUNK])
        act = (jax.nn.silu(g) * up).astype(_BF16)
        acc = acc + _dot(act, w2_ref[lo:lo + FFN_CHUNK, :])
    if final:
        acc = _rmsnorm(acc, fn_ref[...])
    o_ref[...] = acc


def _resident(shape):
    nd = len(shape)
    return pl.BlockSpec(shape, lambda *_: (0,) * nd, pipeline_mode=pl.Buffered(1))


def _smem():
    return pl.BlockSpec(memory_space=pltpu.SMEM)


def _params(semantics):
    return pltpu.CompilerParams(dimension_semantics=semantics, vmem_limit_bytes=VMEM_LIMIT_BYTES)


def _layer_weight_specs():
    return [
        _resident((1, D_MODEL)),
        _resident((D_MODEL, IN_W)),
        _resident((CONF_K, MIX_W)),
        _resident((1, MIX_W)),
        _resident((1, MIX_W)),
        _resident((1, MIX_W)),
        _resident((SC_K, MIX_W)),
        _resident((N_BRANCH, MIX_W, D_MODEL)),
        _resident((D_MODEL, D_MODEL)),
    ]


def _prompt_mixer(x, cos, sin, sinks, lw):
    b, s, _ = x.shape
    t = PROMPT_TILE
    grid = (b, s // t)
    return pl.pallas_call(
        _prompt_mixer_kernel,
        grid=grid,
        in_specs=[_smem(),
                  pl.BlockSpec((1, t, D_MODEL), lambda i, j: (i, j, 0)),
                  pl.BlockSpec((t, LANES), lambda i, j: (j, 0)),
                  pl.BlockSpec((t, LANES), lambda i, j: (j, 0))] + _layer_weight_specs(),
        out_specs=[pl.BlockSpec((1, t, D_MODEL), lambda i, j: (i, j, 0)),
                   pl.BlockSpec((1, WINDOW, KV_W), lambda i, j: (i, 0, 0)),
                   pl.BlockSpec((1, WINDOW, KV_W), lambda i, j: (i, 0, 0)),
                   pl.BlockSpec((1, CONF_HIST, MIX_W), lambda i, j: (i, 0, 0)),
                   pl.BlockSpec((1, SC_HIST, MIX_W), lambda i, j: (i, 0, 0))],
        out_shape=[jax.ShapeDtypeStruct((b, s, D_MODEL), _F32),
                   jax.ShapeDtypeStruct((b, WINDOW, KV_W), _F32),
                   jax.ShapeDtypeStruct((b, WINDOW, KV_W), _F32),
                   jax.ShapeDtypeStruct((b, CONF_HIST, MIX_W), _F32),
                   jax.ShapeDtypeStruct((b, SC_HIST, MIX_W), _F32)],
        scratch_shapes=[pltpu.VMEM((BLOCK, KV_W), _BF16),
                        pltpu.VMEM((BLOCK, KV_W), _BF16),
                        pltpu.VMEM((CONF_HIST + t + SUBLANES, MIX_W), _F32),
                        pltpu.VMEM((SC_HIST + t, MIX_W), _F32)],
        compiler_params=_params(("arbitrary", "arbitrary")),
        name="prompt_mixer",
    )(sinks, x, cos, sin, *lw)


def _sample_mixer(x, cos, sin, ck, cv, sconf, ssc, sinks, lw, ts):
    n_tok = x.shape[0]
    bc = SAMPLE_BATCH_TILE
    r = bc * ts
    nb = ck.shape[0]
    cache_len = ck.shape[1]
    rows = lambda i: (i, 0)
    seqs = lambda i: (i, 0, 0)
    return pl.pallas_call(
        _sample_mixer_kernel,
        grid=(nb // bc,),
        in_specs=[_smem(),
                  pl.BlockSpec((r, D_MODEL), rows),
                  _resident((r, LANES)),
                  _resident((r, LANES)),
                  pl.BlockSpec((bc, cache_len, KV_W), seqs),
                  pl.BlockSpec((bc, cache_len, KV_W), seqs),
                  pl.BlockSpec((bc, CONF_K - 1, MIX_W), seqs),
                  pl.BlockSpec((bc, SC_K - 1, MIX_W), seqs)] + _layer_weight_specs(),
        out_specs=[pl.BlockSpec((r, D_MODEL), rows),
                   pl.BlockSpec((bc, cache_len, KV_W), seqs),
                   pl.BlockSpec((bc, cache_len, KV_W), seqs),
                   pl.BlockSpec((bc, CONF_K - 1, MIX_W), seqs),
                   pl.BlockSpec((bc, SC_K - 1, MIX_W), seqs)],
        out_shape=[jax.ShapeDtypeStruct((n_tok, D_MODEL), _F32),
                   jax.ShapeDtypeStruct(ck.shape, _F32),
                   jax.ShapeDtypeStruct(cv.shape, _F32),
                   jax.ShapeDtypeStruct(sconf.shape, _F32),
                   jax.ShapeDtypeStruct(ssc.shape, _F32)],
        scratch_shapes=[pltpu.VMEM((r, Q_W), _F32),
                        pltpu.VMEM((r, KV_W), _F32),
                        pltpu.VMEM((r, KV_W), _F32),
                        pltpu.VMEM((r, MIX_W), _F32),
                        pltpu.VMEM((r, MIX_W), _F32),
                        pltpu.VMEM((r, Q_W), _F32),
                        pltpu.VMEM((r, MIX_W), _F32),
                        pltpu.VMEM((r, MIX_W), _F32),
                        pltpu.VMEM((CONF_K - 1 + ts, MIX_W), _F32),
                        pltpu.VMEM((SC_K - 1 + ts, MIX_W), _F32)],
        compiler_params=_params(("arbitrary",)),
        name="sample_mixer",
    )(sinks, x, cos, sin, ck, cv, sconf, ssc, *lw)


def _ffn(x, n2, w1, w2, fn, final, name):
    n_tok = x.shape[0]
    t = FFN_TILE
    rows = lambda i: (i, 0)
    return pl.pallas_call(
        functools.partial(_ffn_kernel, final=final),
        grid=(n_tok // t,),
        in_specs=[pl.BlockSpec((t, D_MODEL), rows),
                  _resident((1, D_MODEL)),
                  _resident((D_MODEL, 2 * D_FF)),
                  _resident((D_FF, D_MODEL)),
                  _resident((1, D_MODEL))],
        out_specs=pl.BlockSpec((t, D_MODEL), rows),
        out_shape=jax.ShapeDtypeStruct((n_tok, D_MODEL), _F32),
        compiler_params=_params(("arbitrary",)),
        name=name,
    )(x, n2, w1, w2, fn)


def _rope_tables(positions):
    half = HEAD_DIM // 2
    inv = ROPE_THETA ** (-np.arange(half, dtype=np.float64) / half)
    ang = np.asarray(positions, np.float64)[:, None] * inv[None, :]
    cos = np.cos(ang)
    sin = np.sin(ang)
    cos_t = np.tile(np.concatenate([cos, cos], axis=-1), (1, LANES // HEAD_DIM))
    sin_t = np.tile(np.concatenate([-sin, sin], axis=-1), (1, LANES // HEAD_DIM))
    return jnp.asarray(cos_t, _F32), jnp.asarray(sin_t, _F32)


def kernel(x_prompt, x_sample, cache_k, cache_v, state_conf, state_sconv, norm1, w_in, sinks,
           conf_dw_w, conf_dw_b, conf_ln_g, conf_ln_b, sconv_w, w_branch, w_out, norm2,
           w_ffn_in, w_ffn_out, final_norm):
    depth = w_in.shape[0]
    bp, sp, _ = x_prompt.shape
    bs, ts, _ = x_sample.shape
    cache_len = cache_k.shape[2]
    assert sp % PROMPT_TILE == 0 and PROMPT_TILE % BLOCK == 0 and sp >= WINDOW
    assert bs % SAMPLE_BATCH_TILE == 0 and ts % SUBLANES == 0 and cache_len == WINDOW
    assert (bp * sp) % FFN_TILE == 0 and (bs * ts) % FFN_TILE == 0 and D_FF % FFN_CHUNK == 0

    cos_p, sin_p = _rope_tables(np.arange(sp))
    cos_s, sin_s = _rope_tables(np.tile(PAST_LEN + np.arange(ts), SAMPLE_BATCH_TILE))

    xp = x_prompt
    xs = x_sample.reshape(bs * ts, D_MODEL)
    ck = cache_k.reshape(depth, bs, cache_len, KV_W)
    cv = cache_v.reshape(depth, bs, cache_len, KV_W)
    fn = final_norm.reshape(1, D_MODEL)
    outs = [[] for _ in range(8)]
    for l in range(depth):
        lw = (norm1[l].reshape(1, D_MODEL), w_in[l].astype(_BF16), conf_dw_w[l],
              conf_dw_b[l].reshape(1, MIX_W), conf_ln_g[l].reshape(1, MIX_W),
              conf_ln_b[l].reshape(1, MIX_W), sconv_w[l], w_branch[l].astype(_BF16),
              w_out[l].astype(_BF16))
        n2 = norm2[l].reshape(1, D_MODEL)
        w1 = w_ffn_in[l].astype(_BF16)
        w2 = w_ffn_out[l].astype(_BF16)
        final = l == depth - 1

        xp, kp, vp, cp, scp = _prompt_mixer(xp, cos_p, sin_p, sinks[l], lw)
        xp = _ffn(xp.reshape(bp * sp, D_MODEL), n2, w1, w2, fn, final, "prompt_ffn").reshape(bp, sp, D_MODEL)
        xs, kss, vss, css, scs = _sample_mixer(xs, cos_s, sin_s, ck[l], cv[l], state_conf[l],
                                               state_sconv[l], sinks[l], lw, ts)
        xs = _ffn(xs, n2, w1, w2, fn, final, "sample_ffn")

        outs[0].append(kp.reshape(bp, WINDOW, N_KV_HEADS, HEAD_DIM))
        outs[1].append(vp.reshape(bp, WINDOW, N_KV_HEADS, HEAD_DIM))
        outs[2].append(cp[:, CONF_HIST - (CONF_K - 1):, :])
        outs[3].append(scp[:, SC_HIST - (SC_K - 1):, :])
        outs[4].append(kss.reshape(bs, cache_len, N_KV_HEADS, HEAD_DIM))
        outs[5].append(vss.reshape(bs, cache_len, N_KV_HEADS, HEAD_DIM))
        outs[6].append(css)
        outs[7].append(scs)
    return (xp, xs.reshape(bs, ts, D_MODEL)) + tuple(jnp.stack(o) for o in outs)
```

```python
import functools

import numpy as np
import jax
import jax.numpy as jnp
from jax import lax
from jax.experimental import pallas as pl
from jax.experimental.pallas import tpu as pltpu

D_MODEL = 1024
HEAD_DIM = 64
N_HEADS = 8
N_KV_HEADS = 2
GROUP = N_HEADS // N_KV_HEADS
WINDOW = 128
BLOCK = 128
ROPE_THETA = 10000.0
Q_W = N_HEADS * HEAD_DIM
KV_W = N_KV_HEADS * HEAD_DIM
MIX_W = Q_W
CONF_K = 31
SC_K = 3
N_BRANCH = 3
D_FF = 2816
EPS = 1e-6
PAST_LEN = 16384

_O_Q = 0
_O_K = _O_Q + Q_W
_O_V = _O_K + KV_W
_O_CA = _O_V + KV_W
_O_CG = _O_CA + MIX_W
_O_SB = _O_CG + MIX_W
_O_SC = _O_SB + MIX_W
_O_SX = _O_SC + MIX_W
_O_GL = _O_SX + MIX_W
IN_W = _O_GL + N_BRANCH * D_MODEL

SUBLANES = 8
LANES = 128
VMEM_LIMIT_BYTES = 56 * 1024 * 1024

PROMPT_TILE = 512
SAMPLE_BATCH_TILE = 32
FFN_TILE = 512
FFN_CHUNK = 256
CONF_HIST = 32
SC_HIST = 8

_BF16 = jnp.bfloat16
_F32 = jnp.float32


def _dot(a, b):
    return jnp.dot(a, b, preferred_element_type=_F32)


def _rmsnorm(x, g):
    return x * lax.rsqrt(jnp.mean(x * x, axis=-1, keepdims=True) + EPS) * g


def _rope(x, cos, sin_signed):
    lane = lax.broadcasted_iota(jnp.int32, (x.shape[0], LANES), 1)
    first_half = (lane % HEAD_DIM) < (HEAD_DIM // 2)
    cols = []
    for c in range(x.shape[1] // LANES):
        xb = x[:, c * LANES:(c + 1) * LANES]
        partner = jnp.where(first_half,
                            pltpu.roll(xb, LANES - HEAD_DIM // 2, 1),
                            pltpu.roll(xb, HEAD_DIM // 2, 1))
        cols.append(xb * cos + partner * sin_signed)
    return cols[0] if len(cols) == 1 else jnp.concatenate(cols, axis=-1)


def _sink_attention(q, k, v, mask, sink_col):
    s = lax.dot_general(q, k, (((1,), (1,)), ((), ())), preferred_element_type=_F32)
    s = jnp.where(mask, s, -jnp.inf)
    m = jnp.maximum(jnp.max(s, axis=-1, keepdims=True), sink_col)
    p = jnp.exp(s - m)
    denom = jnp.sum(p, axis=-1, keepdims=True) + jnp.exp(sink_col - m)
    return _dot(p.astype(_BF16), v) / denom


def _layernorm(x, g, b):
    mu = jnp.mean(x, axis=-1, keepdims=True)
    xc = x - mu
    var = jnp.mean(xc * xc, axis=-1, keepdims=True)
    return xc * lax.rsqrt(var + EPS) * g + b


def _conf_conv_rows(ext_ref, w_ref, r0, rows):
    base = CONF_HIST - (CONF_K - 1)
    out = None
    for r in range(SUBLANES):
        g = None
        for a in range((base + CONF_K - 1) // SUBLANES + 1):
            j = a * SUBLANES + r - base
            if j < 0 or j >= CONF_K:
                continue
            lo = r0 + a * SUBLANES
            term = w_ref[j:j + 1, :] * ext_ref[lo:lo + rows + SUBLANES, :]
            g = term if g is None else g + term
        if g is None:
            continue
        part = g[r:r + rows, :]
        out = part if out is None else out + part
    return out


def _branch_merge(h, win_ref, wb_ref, wout_ref, attn_o, conf_o, sc_o):
    merged = None
    for i, br in enumerate((attn_o, conf_o, sc_o)):
        gate = jax.nn.sigmoid(_dot(h, win_ref[:, _O_GL + i * D_MODEL:_O_GL + (i + 1) * D_MODEL]))
        term = gate * _dot(br.astype(_BF16), wb_ref[i])
        merged = term if merged is None else merged + term
    return _dot(merged.astype(_BF16), wout_ref[...])


def _prompt_mixer_kernel(sinks_ref, x_ref, cos_ref, sin_ref, n1_ref, win_ref, dww_ref, dwb_ref,
                         lng_ref, lnb_ref, scw_ref, wb_ref, wout_ref,
                         xo_ref, kst_ref, vst_ref, cst_ref, sst_ref,
                         kprev_ref, vprev_ref, cext_ref, sext_ref):
    t = PROMPT_TILE
    j = pl.program_id(1)

    @pl.when(j == 0)
    def _():
        kprev_ref[...] = jnp.zeros_like(kprev_ref)
        vprev_ref[...] = jnp.zeros_like(vprev_ref)
        cext_ref[...] = jnp.zeros_like(cext_ref)
        sext_ref[...] = jnp.zeros_like(sext_ref)

    x = x_ref[0]
    h = _rmsnorm(x, n1_ref[...]).astype(_BF16)

    qkv = _dot(h, win_ref[:, _O_Q:_O_CA])
    cacg = _dot(h, win_ref[:, _O_CA:_O_SB])

    cos = cos_ref[...]
    sin = sin_ref[...]
    q = _rope(qkv[:, _O_Q:_O_K], cos, sin) * (HEAD_DIM ** -0.5)
    k = _rope(qkv[:, _O_K:_O_V], cos, sin)
    v = qkv[:, _O_V:_O_CA]
    kst_ref[0] = k[t - WINDOW:, :]
    vst_ref[0] = v[t - WINDOW:, :]
    qb = q.astype(_BF16)
    kb = jnp.concatenate([kprev_ref[...], k.astype(_BF16)], axis=0)
    vb = jnp.concatenate([vprev_ref[...], v.astype(_BF16)], axis=0)
    kprev_ref[...] = kb[t:, :]
    vprev_ref[...] = vb[t:, :]

    qi = lax.broadcasted_iota(jnp.int32, (BLOCK, 2 * BLOCK), 0)
    kj = lax.broadcasted_iota(jnp.int32, (BLOCK, 2 * BLOCK), 1)
    diff = qi + BLOCK - kj
    band = (diff >= 0) & (diff < WINDOW)
    band4 = jnp.concatenate([band] * GROUP, axis=0)
    first_key = jnp.where(j == 0, BLOCK, 0)
    first_mask = band4 & jnp.concatenate([kj >= first_key] * GROUP, axis=0)

    sink_cols = [jnp.concatenate(
        [jnp.full((BLOCK, 1), sinks_ref[kh * GROUP + g], _F32) for g in range(GROUP)], axis=0)
        for kh in range(N_KV_HEADS)]

    def scores(blk):
        r0 = blk * BLOCK
        out = []
        for kh in range(N_KV_HEADS):
            qs = jnp.concatenate(
                [qb[r0:r0 + BLOCK, (kh * GROUP + g) * HEAD_DIM:(kh * GROUP + g + 1) * HEAD_DIM]
                 for g in range(GROUP)], axis=0)
            out.append(lax.dot_general(qs, kb[r0:r0 + 2 * BLOCK, kh * HEAD_DIM:(kh + 1) * HEAD_DIM],
                                       (((1,), (1,)), ((), ())), preferred_element_type=_F32))
        return out

    def softmax(blk, s_list):
        mask = first_mask if blk == 0 else band4
        out = []
        for kh in range(N_KV_HEADS):
            s = jnp.where(mask, s_list[kh], -jnp.inf)
            m = jnp.maximum(jnp.max(s, axis=-1, keepdims=True), sink_cols[kh])
            p = jnp.exp(s - m)
            denom = jnp.sum(p, axis=-1, keepdims=True) + jnp.exp(sink_cols[kh] - m)
            out.append((p.astype(_BF16), denom))
        return out

    def weighted_values(blk, p_list):
        r0 = blk * BLOCK
        heads = [None] * N_HEADS
        for kh in range(N_KV_HEADS):
            p, denom = p_list[kh]
            o = _dot(p, vb[r0:r0 + 2 * BLOCK, kh * HEAD_DIM:(kh + 1) * HEAD_DIM]) / denom
            for g in range(GROUP):
                heads[kh * GROUP + g] = o[g * BLOCK:(g + 1) * BLOCK, :]
        return jnp.concatenate(heads, axis=-1)

    n_blk = t // BLOCK

    u = cacg[:, :MIX_W] * jax.nn.sigmoid(cacg[:, MIX_W:])
    cext_ref[CONF_HIST:CONF_HIST + t, :] = u
    cst_ref[0] = u[t - CONF_HIST:, :]
    dw = _conf_conv_rows(cext_ref, dww_ref, 0, t)
    cext_ref[0:CONF_HIST, :] = u[t - CONF_HIST:, :]
    conf_o = jax.nn.silu(_layernorm(dw + dwb_ref[...], lng_ref[...], lnb_ref[...]))

    wide = [(_O_SC, _O_GL), (_O_GL, _O_GL + D_MODEL), (_O_GL + D_MODEL, _O_GL + 2 * D_MODEL),
            (_O_GL + 2 * D_MODEL, _O_GL + 3 * D_MODEL), (_O_SB, _O_SC)]
    wide_out = []
    attn_rows = []
    s_cur = scores(0)
    for blk in range(n_blk):
        if len(wide_out) < len(wide):
            lo, hi = wide[len(wide_out)]
            wide_out.append(_dot(h, win_ref[:, lo:hi]))
        p_cur = softmax(blk, s_cur)
        if blk + 1 < n_blk:
            s_cur = scores(blk + 1)
        attn_rows.append(weighted_values(blk, p_cur))
    while len(wide_out) < len(wide):
        lo, hi = wide[len(wide_out)]
        wide_out.append(_dot(h, win_ref[:, lo:hi]))
    scsx, g0, g1, g2, sb = wide_out
    attn_o = attn_rows[0] if len(attn_rows) == 1 else jnp.concatenate(attn_rows, axis=0)

    cu = scsx[:, :MIX_W] * scsx[:, MIX_W:]
    sext_ref[SC_HIST:SC_HIST + t, :] = cu
    ext = sext_ref[...]
    conv3 = None
    for jj in range(SC_K):
        off = SC_HIST - (SC_K - 1) + jj
        term = scw_ref[jj:jj + 1, :] * ext[off:off + t, :]
        conv3 = term if conv3 is None else conv3 + term
    sst_ref[0] = cu[t - SC_HIST:, :]
    sext_ref[0:SC_HIST, :] = cu[t - SC_HIST:, :]
    sc_o = sb * conv3

    m_sc = jax.nn.sigmoid(g2) * _dot(sc_o.astype(_BF16), wb_ref[2])
    m_attn = jax.nn.sigmoid(g0) * _dot(attn_o.astype(_BF16), wb_ref[0])
    m_conf = jax.nn.sigmoid(g1) * _dot(conf_o.astype(_BF16), wb_ref[1])
    merged = m_attn + m_conf + m_sc
    xo_ref[0] = x + _dot(merged.astype(_BF16), wout_ref[...])


def _sample_mixer_kernel(sinks_ref, x_ref, cos_ref, sin_ref, ck_ref, cv_ref, sconf_ref, ssc_ref,
                         n1_ref, win_ref, dww_ref, dwb_ref, lng_ref, lnb_ref, scw_ref, wb_ref, wout_ref,
                         xo_ref, ko_ref, vo_ref, co_ref, so_ref,
                         q_s, k_s, v_s, u_s, cu_s, attn_s, dw_s, c3_s, cext_s, sext_s):
    bc = SAMPLE_BATCH_TILE
    ts = x_ref.shape[0] // bc
    x = x_ref[...]
    h = _rmsnorm(x, n1_ref[...]).astype(_BF16)
    cos = cos_ref[...]
    sin = sin_ref[...]
    q_s[...] = _rope(_dot(h, win_ref[:, _O_Q:_O_K]), cos, sin) * (HEAD_DIM ** -0.5)
    k_s[...] = _rope(_dot(h, win_ref[:, _O_K:_O_V]), cos, sin)
    v_s[...] = _dot(h, win_ref[:, _O_V:_O_CA])
    u_s[...] = _dot(h, win_ref[:, _O_CA:_O_CG]) * jax.nn.sigmoid(_dot(h, win_ref[:, _O_CG:_O_SB]))
    cu_s[...] = _dot(h, win_ref[:, _O_SC:_O_SX]) * _dot(h, win_ref[:, _O_SX:_O_GL])

    cache_len = ck_ref.shape[1]
    n_keys = cache_len + ts
    tok = lax.broadcasted_iota(jnp.int32, (GROUP * ts, n_keys), 0) % ts
    key = lax.broadcasted_iota(jnp.int32, (GROUP * ts, n_keys), 1)
    diff = jnp.where(key < cache_len, tok + cache_len - key, tok - (key - cache_len))
    mask = (diff >= 0) & (diff < WINDOW)

    def body(b, carry):
        r0 = pl.multiple_of(b * ts, ts)
        rows = pl.ds(r0, ts)
        k_new = k_s[rows, :]
        v_new = v_s[rows, :]
        k_all = jnp.concatenate([ck_ref[b], k_new], axis=0)
        v_all = jnp.concatenate([cv_ref[b], v_new], axis=0)
        ko_ref[b] = k_all[ts:, :]
        vo_ref[b] = v_all[ts:, :]
        kb = k_all.astype(_BF16)
        vb = v_all.astype(_BF16)
        qrow = q_s[rows, :].astype(_BF16)
        heads = [None] * N_HEADS
        for kh in range(N_KV_HEADS):
            qs = jnp.concatenate(
                [qrow[:, (kh * GROUP + g) * HEAD_DIM:(kh * GROUP + g + 1) * HEAD_DIM]
                 for g in range(GROUP)], axis=0)
            sink_col = jnp.concatenate(
                [jnp.full((ts, 1), sinks_ref[kh * GROUP + g], _F32) for g in range(GROUP)], axis=0)
            o = _sink_attention(qs, kb[:, kh * HEAD_DIM:(kh + 1) * HEAD_DIM],
                                vb[:, kh * HEAD_DIM:(kh + 1) * HEAD_DIM], mask, sink_col)
            for g in range(GROUP):
                heads[kh * GROUP + g] = o[g * ts:(g + 1) * ts, :]
        attn_s[rows, :] = jnp.concatenate(heads, axis=-1)

        cext_s[0:CONF_K - 1, :] = sconf_ref[b]
        cext_s[CONF_K - 1:CONF_K - 1 + ts, :] = u_s[rows, :]
        dw = None
        for jj in range(CONF_K):
            term = dww_ref[jj:jj + 1, :] * cext_s[jj:jj + ts, :]
            dw = term if dw is None else dw + term
        dw_s[rows, :] = dw
        co_ref[b] = cext_s[ts:ts + CONF_K - 1, :]

        sext_s[0:SC_K - 1, :] = ssc_ref[b]
        sext_s[SC_K - 1:SC_K - 1 + ts, :] = cu_s[rows, :]
        c3 = None
        for jj in range(SC_K):
            term = scw_ref[jj:jj + 1, :] * sext_s[jj:jj + ts, :]
            c3 = term if c3 is None else c3 + term
        c3_s[rows, :] = c3
        so_ref[b] = sext_s[ts:ts + SC_K - 1, :]
        return carry

    lax.fori_loop(0, bc, body, 0)

    conf_o = jax.nn.silu(_layernorm(dw_s[...] + dwb_ref[...], lng_ref[...], lnb_ref[...]))
    sc_o = _dot(h, win_ref[:, _O_SB:_O_SC]) * c3_s[...]
    xo_ref[...] = x + _branch_merge(h, win_ref, wb_ref, wout_ref, attn_s[...], conf_o, sc_o)


def _ffn_kernel(x_ref, n2_ref, w1_ref, w2_ref, fn_ref, o_ref, *, final):
    x = x_ref[...]
    hn = _rmsnorm(x, n2_ref[...]).astype(_BF16)
    def gate_up(c):
        lo = c * FFN_CHUNK
        return (_dot(hn, w1_ref[:, lo:lo + FFN_CHUNK]),
                _dot(hn, w1_ref[:, D_FF + lo:D_FF + lo + FFN_CHUNK]))

    acc = x
    n_chunks = D_FF // FFN_CHUNK
    g, up = gate_up(0)
    for c in range(n_chunks):
        act = (jax.nn.silu(g) * up).astype(_BF16)
        if c + 1 < n_chunks:
            g, up = gate_up(c + 1)
        acc = acc + _dot(act, w2_ref[c * FFN_CHUNK:(c + 1) * FFN_CHUNK, :])
    if final:
        acc = _rmsnorm(acc, fn_ref[...])
    o_ref[...] = acc


def _resident(shape):
    nd = len(shape)
    return pl.BlockSpec(shape, lambda *_: (0,) * nd, pipeline_mode=pl.Buffered(1))


def _smem():
    return pl.BlockSpec(memory_space=pltpu.SMEM)


def _params(semantics):
    return pltpu.CompilerParams(dimension_semantics=semantics, vmem_limit_bytes=VMEM_LIMIT_BYTES)


def _layer_weight_specs():
    return [
        _resident((1, D_MODEL)),
        _resident((D_MODEL, IN_W)),
        _resident((CONF_K, MIX_W)),
        _resident((1, MIX_W)),
        _resident((1, MIX_W)),
        _resident((1, MIX_W)),
        _resident((SC_K, MIX_W)),
        _resident((N_BRANCH, MIX_W, D_MODEL)),
        _resident((D_MODEL, D_MODEL)),
    ]


def _prompt_mixer(x, cos, sin, sinks, lw):
    b, s, _ = x.shape
    t = PROMPT_TILE
    grid = (b, s // t)
    return pl.pallas_call(
        _prompt_mixer_kernel,
        grid=grid,
        in_specs=[_smem(),
                  pl.BlockSpec((1, t, D_MODEL), lambda i, j: (i, j, 0)),
                  pl.BlockSpec((t, LANES), lambda i, j: (j, 0)),
                  pl.BlockSpec((t, LANES), lambda i, j: (j, 0))] + _layer_weight_specs(),
        out_specs=[pl.BlockSpec((1, t, D_MODEL), lambda i, j: (i, j, 0)),
                   pl.BlockSpec((1, WINDOW, KV_W), lambda i, j: (i, 0, 0)),
                   pl.BlockSpec((1, WINDOW, KV_W), lambda i, j: (i, 0, 0)),
                   pl.BlockSpec((1, CONF_HIST, MIX_W), lambda i, j: (i, 0, 0)),
                   pl.BlockSpec((1, SC_HIST, MIX_W), lambda i, j: (i, 0, 0))],
        out_shape=[jax.ShapeDtypeStruct((b, s, D_MODEL), _F32),
                   jax.ShapeDtypeStruct((b, WINDOW, KV_W), _F32),
                   jax.ShapeDtypeStruct((b, WINDOW, KV_W), _F32),
                   jax.ShapeDtypeStruct((b, CONF_HIST, MIX_W), _F32),
                   jax.ShapeDtypeStruct((b, SC_HIST, MIX_W), _F32)],
        scratch_shapes=[pltpu.VMEM((BLOCK, KV_W), _BF16),
                        pltpu.VMEM((BLOCK, KV_W), _BF16),
                        pltpu.VMEM((CONF_HIST + t + SUBLANES, MIX_W), _F32),
                        pltpu.VMEM((SC_HIST + t, MIX_W), _F32)],
        compiler_params=_params(("arbitrary", "arbitrary")),
        name="prompt_mixer",
    )(sinks, x, cos, sin, *lw)


def _sample_mixer(x, cos, sin, ck, cv, sconf, ssc, sinks, lw, ts):
    n_tok = x.shape[0]
    bc = SAMPLE_BATCH_TILE
    r = bc * ts
    nb = ck.shape[0]
    cache_len = ck.shape[1]
    rows = lambda i: (i, 0)
    seqs = lambda i: (i, 0, 0)
    return pl.pallas_call(
        _sample_mixer_kernel,
        grid=(nb // bc,),
        in_specs=[_smem(),
                  pl.BlockSpec((r, D_MODEL), rows),
                  _resident((r, LANES)),
                  _resident((r, LANES)),
                  pl.BlockSpec((bc, cache_len, KV_W), seqs),
                  pl.BlockSpec((bc, cache_len, KV_W), seqs),
                  pl.BlockSpec((bc, CONF_K - 1, MIX_W), seqs),
                  pl.BlockSpec((bc, SC_K - 1, MIX_W), seqs)] + _layer_weight_specs(),
        out_specs=[pl.BlockSpec((r, D_MODEL), rows),
                   pl.BlockSpec((bc, cache_len, KV_W), seqs),
                   pl.BlockSpec((bc, cache_len, KV_W), seqs),
                   pl.BlockSpec((bc, CONF_K - 1, MIX_W), seqs),
                   pl.BlockSpec((bc, SC_K - 1, MIX_W), seqs)],
        out_shape=[jax.ShapeDtypeStruct((n_tok, D_MODEL), _F32),
                   jax.ShapeDtypeStruct(ck.shape, _F32),
                   jax.ShapeDtypeStruct(cv.shape, _F32),
                   jax.ShapeDtypeStruct(sconf.shape, _F32),
                   jax.ShapeDtypeStruct(ssc.shape, _F32)],
        scratch_shapes=[pltpu.VMEM((r, Q_W), _F32),
                        pltpu.VMEM((r, KV_W), _F32),
                        pltpu.VMEM((r, KV_W), _F32),
                        pltpu.VMEM((r, MIX_W), _F32),
                        pltpu.VMEM((r, MIX_W), _F32),
                        pltpu.VMEM((r, Q_W), _F32),
                        pltpu.VMEM((r, MIX_W), _F32),
                        pltpu.VMEM((r, MIX_W), _F32),
                        pltpu.VMEM((CONF_K - 1 + ts, MIX_W), _F32),
                        pltpu.VMEM((SC_K - 1 + ts, MIX_W), _F32)],
        compiler_params=_params(("arbitrary",)),
        name="sample_mixer",
    )(sinks, x, cos, sin, ck, cv, sconf, ssc, *lw)


def _ffn(x, n2, w1, w2, fn, final, name):
    n_tok = x.shape[0]
    t = FFN_TILE
    rows = lambda i: (i, 0)
    return pl.pallas_call(
        functools.partial(_ffn_kernel, final=final),
        grid=(n_tok // t,),
        in_specs=[pl.BlockSpec((t, D_MODEL), rows),
                  _resident((1, D_MODEL)),
                  _resident((D_MODEL, 2 * D_FF)),
                  _resident((D_FF, D_MODEL)),
                  _resident((1, D_MODEL))],
        out_specs=pl.BlockSpec((t, D_MODEL), rows),
        out_shape=jax.ShapeDtypeStruct((n_tok, D_MODEL), _F32),
        compiler_params=_params(("arbitrary",)),
        name=name,
    )(x, n2, w1, w2, fn)


def _rope_tables(positions):
    half = HEAD_DIM // 2
    inv = ROPE_THETA ** (-np.arange(half, dtype=np.float64) / half)
    ang = np.asarray(positions, np.float64)[:, None] * inv[None, :]
    cos = np.cos(ang)
    sin = np.sin(ang)
    cos_t = np.tile(np.concatenate([cos, cos], axis=-1), (1, LANES // HEAD_DIM))
    sin_t = np.tile(np.concatenate([-sin, sin], axis=-1), (1, LANES // HEAD_DIM))
    return jnp.asarray(cos_t, _F32), jnp.asarray(sin_t, _F32)


def kernel(x_prompt, x_sample, cache_k, cache_v, state_conf, state_sconv, norm1, w_in, sinks,
           conf_dw_w, conf_dw_b, conf_ln_g, conf_ln_b, sconv_w, w_branch, w_out, norm2,
           w_ffn_in, w_ffn_out, final_norm):
    depth = w_in.shape[0]
    bp, sp, _ = x_prompt.shape
    bs, ts, _ = x_sample.shape
    cache_len = cache_k.shape[2]
    assert sp % PROMPT_TILE == 0 and PROMPT_TILE % BLOCK == 0 and sp >= WINDOW
    assert bs % SAMPLE_BATCH_TILE == 0 and ts % SUBLANES == 0 and cache_len == WINDOW
    assert (bp * sp) % FFN_TILE == 0 and (bs * ts) % FFN_TILE == 0 and D_FF % FFN_CHUNK == 0

    cos_p, sin_p = _rope_tables(np.arange(sp))
    cos_s, sin_s = _rope_tables(np.tile(PAST_LEN + np.arange(ts), SAMPLE_BATCH_TILE))

    xp = x_prompt
    xs = x_sample.reshape(bs * ts, D_MODEL)
    ck = cache_k.reshape(depth, bs, cache_len, KV_W)
    cv = cache_v.reshape(depth, bs, cache_len, KV_W)
    fn = final_norm.reshape(1, D_MODEL)
    outs = [[] for _ in range(8)]
    for l in range(depth):
        lw = (norm1[l].reshape(1, D_MODEL), w_in[l].astype(_BF16), conf_dw_w[l],
              conf_dw_b[l].reshape(1, MIX_W), conf_ln_g[l].reshape(1, MIX_W),
              conf_ln_b[l].reshape(1, MIX_W), sconv_w[l], w_branch[l].astype(_BF16),
              w_out[l].astype(_BF16))
        n2 = norm2[l].reshape(1, D_MODEL)
        w1 = w_ffn_in[l].astype(_BF16)
        w2 = w_ffn_out[l].astype(_BF16)
        final = l == depth - 1

        xp, kp, vp, cp, scp = _prompt_mixer(xp, cos_p, sin_p, sinks[l], lw)
        xp = _ffn(xp.reshape(bp * sp, D_MODEL), n2, w1, w2, fn, final, "prompt_ffn").reshape(bp, sp, D_MODEL)
        xs, kss, vss, css, scs = _sample_mixer(xs, cos_s, sin_s, ck[l], cv[l], state_conf[l],
                                               state_sconv[l], sinks[l], lw, ts)
        xs = _ffn(xs, n2, w1, w2, fn, final, "sample_ffn")

        outs[0].append(kp.reshape(bp, WINDOW, N_KV_HEADS, HEAD_DIM))
        outs[1].append(vp.reshape(bp, WINDOW, N_KV_HEADS, HEAD_DIM))
        outs[2].append(cp[:, CONF_HIST - (CONF_K - 1):, :])
        outs[3].append(scp[:, SC_HIST - (SC_K - 1):, :])
        outs[4].append(kss.reshape(bs, cache_len, N_KV_HEADS, HEAD_DIM))
        outs[5].append(vss.reshape(bs, cache_len, N_KV_HEADS, HEAD_DIM))
        outs[6].append(css)
        outs[7].append(scs)
    return (xp, xs.reshape(bs, ts, D_MODEL)) + tuple(jnp.stack(o) for o in outs)
```

```python
import functools

import numpy as np
import jax
import jax.numpy as jnp
from jax import lax
from jax.experimental import pallas as pl
from jax.experimental.pallas import tpu as pltpu

D_MODEL = 1024
HEAD_DIM = 64
N_HEADS = 8
N_KV_HEADS = 2
GROUP = N_HEADS // N_KV_HEADS
WINDOW = 128
BLOCK = 128
ROPE_THETA = 10000.0
Q_W = N_HEADS * HEAD_DIM
KV_W = N_KV_HEADS * HEAD_DIM
MIX_W = Q_W
CONF_K = 31
SC_K = 3
N_BRANCH = 3
D_FF = 2816
EPS = 1e-6
PAST_LEN = 16384

_O_Q = 0
_O_K = _O_Q + Q_W
_O_V = _O_K + KV_W
_O_CA = _O_V + KV_W
_O_CG = _O_CA + MIX_W
_O_SB = _O_CG + MIX_W
_O_SC = _O_SB + MIX_W
_O_SX = _O_SC + MIX_W
_O_GL = _O_SX + MIX_W
IN_W = _O_GL + N_BRANCH * D_MODEL

SUBLANES = 8
LANES = 128
VMEM_LIMIT_BYTES = 56 * 1024 * 1024

PROMPT_TILE = 512
SAMPLE_BATCH_TILE = 32
SAMPLE_UNROLL = 4
FFN_TILE = 512
FFN_CHUNK = 256
CONF_HIST = 32
SC_HIST = 8

_BF16 = jnp.bfloat16
_F32 = jnp.float32


def _dot(a, b):
    return jnp.dot(a, b, preferred_element_type=_F32)


def _rmsnorm(x, g):
    return x * lax.rsqrt(jnp.mean(x * x, axis=-1, keepdims=True) + EPS) * g


def _rope(x, cos, sin_signed):
    lane = lax.broadcasted_iota(jnp.int32, (x.shape[0], LANES), 1)
    first_half = (lane % HEAD_DIM) < (HEAD_DIM // 2)
    cols = []
    for c in range(x.shape[1] // LANES):
        xb = x[:, c * LANES:(c + 1) * LANES]
        partner = jnp.where(first_half,
                            pltpu.roll(xb, LANES - HEAD_DIM // 2, 1),
                            pltpu.roll(xb, HEAD_DIM // 2, 1))
        cols.append(xb * cos + partner * sin_signed)
    return cols[0] if len(cols) == 1 else jnp.concatenate(cols, axis=-1)


def _sink_attention(q, k, v, mask, sink_col):
    s = lax.dot_general(q, k, (((1,), (1,)), ((), ())), preferred_element_type=_F32)
    s = jnp.where(mask, s, -jnp.inf)
    m = jnp.maximum(jnp.max(s, axis=-1, keepdims=True), sink_col)
    p = jnp.exp(s - m)
    denom = jnp.sum(p, axis=-1, keepdims=True) + jnp.exp(sink_col - m)
    return _dot(p.astype(_BF16), v) / denom


def _layernorm(x, g, b):
    mu = jnp.mean(x, axis=-1, keepdims=True)
    xc = x - mu
    var = jnp.mean(xc * xc, axis=-1, keepdims=True)
    return xc * lax.rsqrt(var + EPS) * g + b


def _conf_conv_rows(ext_ref, w_ref, r0, rows):
    base = CONF_HIST - (CONF_K - 1)
    out = None
    for r in range(SUBLANES):
        g = None
        for a in range((base + CONF_K - 1) // SUBLANES + 1):
            j = a * SUBLANES + r - base
            if j < 0 or j >= CONF_K:
                continue
            lo = r0 + a * SUBLANES
            term = w_ref[j:j + 1, :] * ext_ref[lo:lo + rows + SUBLANES, :]
            g = term if g is None else g + term
        if g is None:
            continue
        part = g[r:r + rows, :]
        out = part if out is None else out + part
    return out


def _branch_merge(h, win_ref, wb_ref, wout_ref, attn_o, conf_o, sc_o):
    merged = None
    for i, br in enumerate((attn_o, conf_o, sc_o)):
        gate = jax.nn.sigmoid(_dot(h, win_ref[:, _O_GL + i * D_MODEL:_O_GL + (i + 1) * D_MODEL]))
        term = gate * _dot(br.astype(_BF16), wb_ref[i])
        merged = term if merged is None else merged + term
    return _dot(merged.astype(_BF16), wout_ref[...])


def _prompt_mixer_kernel(sinks_ref, x_ref, cos_ref, sin_ref, n1_ref, win_ref, dww_ref, dwb_ref,
                         lng_ref, lnb_ref, scw_ref, wb_ref, wout_ref,
                         xo_ref, kst_ref, vst_ref, cst_ref, sst_ref,
                         kprev_ref, vprev_ref, cext_ref, sext_ref):
    t = PROMPT_TILE
    j = pl.program_id(1)

    @pl.when(j == 0)
    def _():
        kprev_ref[...] = jnp.zeros_like(kprev_ref)
        vprev_ref[...] = jnp.zeros_like(vprev_ref)
        cext_ref[...] = jnp.zeros_like(cext_ref)
        sext_ref[...] = jnp.zeros_like(sext_ref)

    x = x_ref[0]
    h = _rmsnorm(x, n1_ref[...]).astype(_BF16)

    qkv = _dot(h, win_ref[:, _O_Q:_O_CA])
    cacg = _dot(h, win_ref[:, _O_CA:_O_SB])

    cos = cos_ref[...]
    sin = sin_ref[...]
    q = _rope(qkv[:, _O_Q:_O_K], cos, sin) * (HEAD_DIM ** -0.5)
    k = _rope(qkv[:, _O_K:_O_V], cos, sin)
    v = qkv[:, _O_V:_O_CA]
    kst_ref[0] = k[t - WINDOW:, :]
    vst_ref[0] = v[t - WINDOW:, :]
    qb = q.astype(_BF16)
    kb = jnp.concatenate([kprev_ref[...], k.astype(_BF16)], axis=0)
    vb = jnp.concatenate([vprev_ref[...], v.astype(_BF16)], axis=0)
    kprev_ref[...] = kb[t:, :]
    vprev_ref[...] = vb[t:, :]

    qi = lax.broadcasted_iota(jnp.int32, (BLOCK, 2 * BLOCK), 0)
    kj = lax.broadcasted_iota(jnp.int32, (BLOCK, 2 * BLOCK), 1)
    diff = qi + BLOCK - kj
    band = (diff >= 0) & (diff < WINDOW)
    band4 = jnp.concatenate([band] * GROUP, axis=0)
    first_key = jnp.where(j == 0, BLOCK, 0)
    first_mask = band4 & jnp.concatenate([kj >= first_key] * GROUP, axis=0)

    sink_cols = [jnp.concatenate(
        [jnp.full((BLOCK, 1), sinks_ref[kh * GROUP + g], _F32) for g in range(GROUP)], axis=0)
        for kh in range(N_KV_HEADS)]

    def scores(blk):
        r0 = blk * BLOCK
        out = []
        for kh in range(N_KV_HEADS):
            qs = jnp.concatenate(
                [qb[r0:r0 + BLOCK, (kh * GROUP + g) * HEAD_DIM:(kh * GROUP + g + 1) * HEAD_DIM]
                 for g in range(GROUP)], axis=0)
            out.append(lax.dot_general(qs, kb[r0:r0 + 2 * BLOCK, kh * HEAD_DIM:(kh + 1) * HEAD_DIM],
                                       (((1,), (1,)), ((), ())), preferred_element_type=_F32))
        return out

    def softmax(blk, s_list):
        mask = first_mask if blk == 0 else band4
        out = []
        for kh in range(N_KV_HEADS):
            s = jnp.where(mask, s_list[kh], -jnp.inf)
            m = jnp.maximum(jnp.max(s, axis=-1, keepdims=True), sink_cols[kh])
            p = jnp.exp(s - m)
            denom = jnp.sum(p, axis=-1, keepdims=True) + jnp.exp(sink_cols[kh] - m)
            out.append((p.astype(_BF16), denom))
        return out

    def weighted_values(blk, p_list):
        r0 = blk * BLOCK
        heads = [None] * N_HEADS
        for kh in range(N_KV_HEADS):
            p, denom = p_list[kh]
            o = _dot(p, vb[r0:r0 + 2 * BLOCK, kh * HEAD_DIM:(kh + 1) * HEAD_DIM]) / denom
            for g in range(GROUP):
                heads[kh * GROUP + g] = o[g * BLOCK:(g + 1) * BLOCK, :]
        return jnp.concatenate(heads, axis=-1)

    n_blk = t // BLOCK

    u = cacg[:, :MIX_W] * jax.nn.sigmoid(cacg[:, MIX_W:])
    cext_ref[CONF_HIST:CONF_HIST + t, :] = u
    cst_ref[0] = u[t - CONF_HIST:, :]
    dw = _conf_conv_rows(cext_ref, dww_ref, 0, t)
    cext_ref[0:CONF_HIST, :] = u[t - CONF_HIST:, :]
    conf_o = jax.nn.silu(_layernorm(dw + dwb_ref[...], lng_ref[...], lnb_ref[...]))

    wide = [(_O_SC, _O_GL), (_O_GL, _O_GL + D_MODEL), (_O_GL + D_MODEL, _O_GL + 2 * D_MODEL),
            (_O_GL + 2 * D_MODEL, _O_GL + 3 * D_MODEL), (_O_SB, _O_SC)]
    wide_out = []
    attn_rows = []
    s_cur = scores(0)
    for blk in range(n_blk):
        if len(wide_out) < len(wide):
            lo, hi = wide[len(wide_out)]
            wide_out.append(_dot(h, win_ref[:, lo:hi]))
        p_cur = softmax(blk, s_cur)
        if blk + 1 < n_blk:
            s_cur = scores(blk + 1)
        attn_rows.append(weighted_values(blk, p_cur))
    while len(wide_out) < len(wide):
        lo, hi = wide[len(wide_out)]
        wide_out.append(_dot(h, win_ref[:, lo:hi]))
    scsx, g0, g1, g2, sb = wide_out
    attn_o = attn_rows[0] if len(attn_rows) == 1 else jnp.concatenate(attn_rows, axis=0)

    cu = scsx[:, :MIX_W] * scsx[:, MIX_W:]
    sext_ref[SC_HIST:SC_HIST + t, :] = cu
    ext = sext_ref[...]
    conv3 = None
    for jj in range(SC_K):
        off = SC_HIST - (SC_K - 1) + jj
        term = scw_ref[jj:jj + 1, :] * ext[off:off + t, :]
        conv3 = term if conv3 is None else conv3 + term
    sst_ref[0] = cu[t - SC_HIST:, :]
    sext_ref[0:SC_HIST, :] = cu[t - SC_HIST:, :]
    sc_o = sb * conv3

    m_sc = jax.nn.sigmoid(g2) * _dot(sc_o.astype(_BF16), wb_ref[2])
    m_attn = jax.nn.sigmoid(g0) * _dot(attn_o.astype(_BF16), wb_ref[0])
    m_conf = jax.nn.sigmoid(g1) * _dot(conf_o.astype(_BF16), wb_ref[1])
    merged = m_attn + m_conf + m_sc
    xo_ref[0] = x + _dot(merged.astype(_BF16), wout_ref[...])


def _sample_mixer_kernel(*refs, n_carried):
    (sinks_ref, x_ref, cos_ref, sin_ref, ck_ref, cv_ref, sconf_ref, ssc_ref,
     n1_ref, win_ref, dww_ref, dwb_ref, lng_ref, lnb_ref, scw_ref, wb_ref, wout_ref) = refs[:17]
    (xo_ref, ko_ref, vo_ref, co_ref, so_ref,
     q_s, k_s, v_s, u_s, cu_s, attn_s, dw_s, c3_s, cext_s, sext_s) = refs[17 + n_carried:]
    bc = SAMPLE_BATCH_TILE
    ts = x_ref.shape[0] // bc
    x = x_ref[...]
    h = _rmsnorm(x, n1_ref[...]).astype(_BF16)
    cos = cos_ref[...]
    sin = sin_ref[...]
    qkv = _dot(h, win_ref[:, _O_Q:_O_CA])
    cacg = _dot(h, win_ref[:, _O_CA:_O_SB])
    scsx = _dot(h, win_ref[:, _O_SC:_O_GL])
    q_s[...] = _rope(qkv[:, _O_Q:_O_K], cos, sin) * (HEAD_DIM ** -0.5)
    k_s[...] = _rope(qkv[:, _O_K:_O_V], cos, sin)
    v_s[...] = qkv[:, _O_V:_O_CA]
    u_s[...] = cacg[:, :MIX_W] * jax.nn.sigmoid(cacg[:, MIX_W:])
    cu_s[...] = scsx[:, :MIX_W] * scsx[:, MIX_W:]

    cache_len = ck_ref.shape[1]
    n_keys = cache_len + ts
    tok = lax.broadcasted_iota(jnp.int32, (GROUP * ts, n_keys), 0) % ts
    key = lax.broadcasted_iota(jnp.int32, (GROUP * ts, n_keys), 1)
    diff = jnp.where(key < cache_len, tok + cache_len - key, tok - (key - cache_len))
    mask = (diff >= 0) & (diff < WINDOW)
    sink_cols = [jnp.concatenate(
        [jnp.full((ts, 1), sinks_ref[kh * GROUP + g], _F32) for g in range(GROUP)], axis=0)
        for kh in range(N_KV_HEADS)]

    def attention(bs_):
        rows = [pl.ds(pl.multiple_of(b * ts, ts), ts) for b in bs_]
        vbs, scores = [], []
        for b, rw in zip(bs_, rows):
            k_all = jnp.concatenate([ck_ref[b], k_s[rw, :]], axis=0)
            v_all = jnp.concatenate([cv_ref[b], v_s[rw, :]], axis=0)
            ko_ref[b] = k_all[ts:, :]
            vo_ref[b] = v_all[ts:, :]
            kb = k_all.astype(_BF16)
            vbs.append(v_all.astype(_BF16))
            qrow = q_s[rw, :].astype(_BF16)
            for kh in range(N_KV_HEADS):
                qs = jnp.concatenate(
                    [qrow[:, (kh * GROUP + g) * HEAD_DIM:(kh * GROUP + g + 1) * HEAD_DIM]
                     for g in range(GROUP)], axis=0)
                scores.append(lax.dot_general(qs, kb[:, kh * HEAD_DIM:(kh + 1) * HEAD_DIM],
                                              (((1,), (1,)), ((), ())), preferred_element_type=_F32))
        probs = []
        for i, s in enumerate(scores):
            sink_col = sink_cols[i % N_KV_HEADS]
            s = jnp.where(mask, s, -jnp.inf)
            m = jnp.maximum(jnp.max(s, axis=-1, keepdims=True), sink_col)
            p = jnp.exp(s - m)
            probs.append((p.astype(_BF16), jnp.sum(p, axis=-1, keepdims=True) + jnp.exp(sink_col - m)))
        for n, rw in enumerate(rows):
            heads = [None] * N_HEADS
            for kh in range(N_KV_HEADS):
                p, denom = probs[n * N_KV_HEADS + kh]
                o = _dot(p, vbs[n][:, kh * HEAD_DIM:(kh + 1) * HEAD_DIM]) / denom
                for g in range(GROUP):
                    heads[kh * GROUP + g] = o[g * ts:(g + 1) * ts, :]
            attn_s[rw, :] = jnp.concatenate(heads, axis=-1)

    def convs(b, slot):
        rows = pl.ds(pl.multiple_of(b * ts, ts), ts)
        cext = cext_s.at[slot]
        cext[0:CONF_K - 1, :] = sconf_ref[b]
        cext[CONF_K - 1:CONF_K - 1 + ts, :] = u_s[rows, :]
        dw = None
        for jj in range(CONF_K):
            term = dww_ref[jj:jj + 1, :] * cext[jj:jj + ts, :]
            dw = term if dw is None else dw + term
        dw_s[rows, :] = dw
        co_ref[b] = cext[ts:ts + CONF_K - 1, :]

        sext = sext_s.at[slot]
        sext[0:SC_K - 1, :] = ssc_ref[b]
        sext[SC_K - 1:SC_K - 1 + ts, :] = cu_s[rows, :]
        c3 = None
        for jj in range(SC_K):
            term = scw_ref[jj:jj + 1, :] * sext[jj:jj + ts, :]
            c3 = term if c3 is None else c3 + term
        c3_s[rows, :] = c3
        so_ref[b] = sext[ts:ts + SC_K - 1, :]

    def body(i, carry):
        group = [i * SAMPLE_UNROLL + slot for slot in range(SAMPLE_UNROLL)]
        attention(group)
        for slot, b in enumerate(group):
            convs(b, slot)
        return carry

    lax.fori_loop(0, bc // SAMPLE_UNROLL, body, 0)

    conf_o = jax.nn.silu(_layernorm(dw_s[...] + dwb_ref[...], lng_ref[...], lnb_ref[...]))
    sc_o = _dot(h, win_ref[:, _O_SB:_O_SC]) * c3_s[...]
    xo_ref[...] = x + _branch_merge(h, win_ref, wb_ref, wout_ref, attn_s[...], conf_o, sc_o)


def _ffn_kernel(x_ref, n2_ref, w1_ref, w2_ref, fn_ref, o_ref, *, final):
    x = x_ref[...]
    hn = _rmsnorm(x, n2_ref[...]).astype(_BF16)

    def gate_up(c):
        lo = c * FFN_CHUNK
        return (_dot(hn, w1_ref[:, lo:lo + FFN_CHUNK]),
                _dot(hn, w1_ref[:, D_FF + lo:D_FF + lo + FFN_CHUNK]))

    acc = x
    n_chunks = D_FF // FFN_CHUNK
    g, up = gate_up(0)
    for c in range(n_chunks):
        act = (jax.nn.silu(g) * up).astype(_BF16)
        if c + 1 < n_chunks:
            g, up = gate_up(c + 1)
        acc = acc + _dot(act, w2_ref[c * FFN_CHUNK:(c + 1) * FFN_CHUNK, :])
    if final:
        acc = _rmsnorm(acc, fn_ref[...])
    o_ref[...] = acc


def _resident(shape):
    nd = len(shape)
    return pl.BlockSpec(shape, lambda *_: (0,) * nd, pipeline_mode=pl.Buffered(1))


def _smem():
    return pl.BlockSpec(memory_space=pltpu.SMEM)


def _params(semantics):
    return pltpu.CompilerParams(dimension_semantics=semantics, vmem_limit_bytes=VMEM_LIMIT_BYTES)


def _layer_weight_specs():
    return [
        _resident((1, D_MODEL)),
        _resident((D_MODEL, IN_W)),
        _resident((CONF_K, MIX_W)),
        _resident((1, MIX_W)),
        _resident((1, MIX_W)),
        _resident((1, MIX_W)),
        _resident((SC_K, MIX_W)),
        _resident((N_BRANCH, MIX_W, D_MODEL)),
        _resident((D_MODEL, D_MODEL)),
    ]


def _prompt_mixer(x, cos, sin, sinks, lw):
    b, s, _ = x.shape
    t = PROMPT_TILE
    grid = (b, s // t)
    return pl.pallas_call(
        _prompt_mixer_kernel,
        grid=grid,
        in_specs=[_smem(),
                  pl.BlockSpec((1, t, D_MODEL), lambda i, j: (i, j, 0)),
                  pl.BlockSpec((t, LANES), lambda i, j: (j, 0)),
                  pl.BlockSpec((t, LANES), lambda i, j: (j, 0))] + _layer_weight_specs(),
        out_specs=[pl.BlockSpec((1, t, D_MODEL), lambda i, j: (i, j, 0)),
                   pl.BlockSpec((1, WINDOW, KV_W), lambda i, j: (i, 0, 0)),
                   pl.BlockSpec((1, WINDOW, KV_W), lambda i, j: (i, 0, 0)),
                   pl.BlockSpec((1, CONF_HIST, MIX_W), lambda i, j: (i, 0, 0)),
                   pl.BlockSpec((1, SC_HIST, MIX_W), lambda i, j: (i, 0, 0))],
        out_shape=[jax.ShapeDtypeStruct((b, s, D_MODEL), _F32),
                   jax.ShapeDtypeStruct((b, WINDOW, KV_W), _F32),
                   jax.ShapeDtypeStruct((b, WINDOW, KV_W), _F32),
                   jax.ShapeDtypeStruct((b, CONF_HIST, MIX_W), _F32),
                   jax.ShapeDtypeStruct((b, SC_HIST, MIX_W), _F32)],
        scratch_shapes=[pltpu.VMEM((BLOCK, KV_W), _BF16),
                        pltpu.VMEM((BLOCK, KV_W), _BF16),
                        pltpu.VMEM((CONF_HIST + t + SUBLANES, MIX_W), _F32),
                        pltpu.VMEM((SC_HIST + t, MIX_W), _F32)],
        compiler_params=_params(("arbitrary", "arbitrary")),
        name="prompt_mixer",
    )(sinks, x, cos, sin, *lw)


def _sample_mixer(layer, x, cos, sin, ck, cv, sconf, ssc, sinks, lw, carried):
    n_tok = x.shape[0]
    bc = SAMPLE_BATCH_TILE
    nb, cache_len = ck.shape[1], ck.shape[2]
    ts = n_tok // nb
    r = bc * ts
    rows = lambda i: (i, 0)
    seqs = lambda i: (layer, i, 0, 0)
    state_specs = [pl.BlockSpec((None, bc, cache_len, KV_W), seqs),
                   pl.BlockSpec((None, bc, cache_len, KV_W), seqs),
                   pl.BlockSpec((None, bc, CONF_K - 1, MIX_W), seqs),
                   pl.BlockSpec((None, bc, SC_K - 1, MIX_W), seqs)]
    n_fixed_in = 8 + len(lw)
    return pl.pallas_call(
        functools.partial(_sample_mixer_kernel, n_carried=len(carried)),
        grid=(nb // bc,),
        in_specs=[_smem(),
                  pl.BlockSpec((r, D_MODEL), rows),
                  _resident((r, LANES)),
                  _resident((r, LANES))] + state_specs + _layer_weight_specs()
                 + [pl.BlockSpec(memory_space=pl.ANY)] * len(carried),
        out_specs=[pl.BlockSpec((r, D_MODEL), rows)] + state_specs,
        out_shape=[jax.ShapeDtypeStruct((n_tok, D_MODEL), _F32),
                   jax.ShapeDtypeStruct(ck.shape, _F32),
                   jax.ShapeDtypeStruct(cv.shape, _F32),
                   jax.ShapeDtypeStruct(sconf.shape, _F32),
                   jax.ShapeDtypeStruct(ssc.shape, _F32)],
        input_output_aliases={n_fixed_in + i: 1 + i for i in range(len(carried))},
        scratch_shapes=[pltpu.VMEM((r, Q_W), _F32),
                        pltpu.VMEM((r, KV_W), _F32),
                        pltpu.VMEM((r, KV_W), _F32),
                        pltpu.VMEM((r, MIX_W), _F32),
                        pltpu.VMEM((r, MIX_W), _F32),
                        pltpu.VMEM((r, Q_W), _F32),
                        pltpu.VMEM((r, MIX_W), _F32),
                        pltpu.VMEM((r, MIX_W), _F32),
                        pltpu.VMEM((SAMPLE_UNROLL, CONF_K - 1 + ts, MIX_W), _F32),
                        pltpu.VMEM((SAMPLE_UNROLL, SC_K - 1 + ts, MIX_W), _F32)],
        compiler_params=_params(("arbitrary",)),
        name="sample_mixer",
    )(sinks, x, cos, sin, ck, cv, sconf, ssc, *lw, *carried)


def _ffn(x, n2, w1, w2, fn, final, name):
    n_tok = x.shape[0]
    t = FFN_TILE
    rows = lambda i: (i, 0)
    return pl.pallas_call(
        functools.partial(_ffn_kernel, final=final),
        grid=(n_tok // t,),
        in_specs=[pl.BlockSpec((t, D_MODEL), rows),
                  _resident((1, D_MODEL)),
                  _resident((D_MODEL, 2 * D_FF)),
                  _resident((D_FF, D_MODEL)),
                  _resident((1, D_MODEL))],
        out_specs=pl.BlockSpec((t, D_MODEL), rows),
        out_shape=jax.ShapeDtypeStruct((n_tok, D_MODEL), _F32),
        compiler_params=_params(("arbitrary",)),
        name=name,
    )(x, n2, w1, w2, fn)


def _rope_tables(positions):
    half = HEAD_DIM // 2
    inv = ROPE_THETA ** (-np.arange(half, dtype=np.float64) / half)
    ang = np.asarray(positions, np.float64)[:, None] * inv[None, :]
    cos = np.cos(ang)
    sin = np.sin(ang)
    cos_t = np.tile(np.concatenate([cos, cos], axis=-1), (1, LANES // HEAD_DIM))
    sin_t = np.tile(np.concatenate([-sin, sin], axis=-1), (1, LANES // HEAD_DIM))
    return jnp.asarray(cos_t, _F32), jnp.asarray(sin_t, _F32)


def kernel(x_prompt, x_sample, cache_k, cache_v, state_conf, state_sconv, norm1, w_in, sinks,
           conf_dw_w, conf_dw_b, conf_ln_g, conf_ln_b, sconv_w, w_branch, w_out, norm2,
           w_ffn_in, w_ffn_out, final_norm):
    depth = w_in.shape[0]
    bp, sp, _ = x_prompt.shape
    bs, ts, _ = x_sample.shape
    cache_len = cache_k.shape[2]
    assert sp % PROMPT_TILE == 0 and PROMPT_TILE % BLOCK == 0 and sp >= WINDOW
    assert bs % SAMPLE_BATCH_TILE == 0 and SAMPLE_BATCH_TILE % SAMPLE_UNROLL == 0
    assert ts % SUBLANES == 0 and cache_len == WINDOW
    assert (bp * sp) % FFN_TILE == 0 and (bs * ts) % FFN_TILE == 0 and D_FF % FFN_CHUNK == 0

    cos_p, sin_p = _rope_tables(np.arange(sp))
    cos_s, sin_s = _rope_tables(np.tile(PAST_LEN + np.arange(ts), SAMPLE_BATCH_TILE))

    xp = x_prompt
    xs = x_sample.reshape(bs * ts, D_MODEL)
    ck = cache_k.reshape(depth, bs, cache_len, KV_W)
    cv = cache_v.reshape(depth, bs, cache_len, KV_W)
    fn = final_norm.reshape(1, D_MODEL)
    prompt_states = [[] for _ in range(4)]
    sample_states = tuple(jnp.zeros(a.shape, _F32) for a in (ck, cv, state_conf, state_sconv))
    for l in range(depth):
        lw = (norm1[l].reshape(1, D_MODEL), w_in[l].astype(_BF16), conf_dw_w[l],
              conf_dw_b[l].reshape(1, MIX_W), conf_ln_g[l].reshape(1, MIX_W),
              conf_ln_b[l].reshape(1, MIX_W), sconv_w[l], w_branch[l].astype(_BF16),
              w_out[l].astype(_BF16))
        n2 = norm2[l].reshape(1, D_MODEL)
        w1 = w_ffn_in[l].astype(_BF16)
        w2 = w_ffn_out[l].astype(_BF16)
        final = l == depth - 1

        xp, kp, vp, cp, scp = _prompt_mixer(xp, cos_p, sin_p, sinks[l], lw)
        xp = _ffn(xp.reshape(bp * sp, D_MODEL), n2, w1, w2, fn, final, "prompt_ffn").reshape(bp, sp, D_MODEL)
        xs, *sample_states = _sample_mixer(l, xs, cos_s, sin_s, ck, cv, state_conf, state_sconv,
                                           sinks[l], lw, tuple(sample_states))
        xs = _ffn(xs, n2, w1, w2, fn, final, "sample_ffn")

        prompt_states[0].append(kp.reshape(bp, WINDOW, N_KV_HEADS, HEAD_DIM))
        prompt_states[1].append(vp.reshape(bp, WINDOW, N_KV_HEADS, HEAD_DIM))
        prompt_states[2].append(cp[:, CONF_HIST - (CONF_K - 1):, :])
        prompt_states[3].append(scp[:, SC_HIST - (SC_K - 1):, :])
    k_s, v_s, c_s, s_s = sample_states
    kv_shape = (depth, bs, cache_len, N_KV_HEADS, HEAD_DIM)
    return ((xp, xs.reshape(bs, ts, D_MODEL)) + tuple(jnp.stack(o) for o in prompt_states)
            + (k_s.reshape(kv_shape), v_s.reshape(kv_shape), c_s, s_s))
```

```python
import functools

import numpy as np
import jax
import jax.numpy as jnp
from jax import lax
from jax.experimental import pallas as pl
from jax.experimental.pallas import tpu as pltpu

D_MODEL = 1024
HEAD_DIM = 64
N_HEADS = 8
N_KV_HEADS = 2
GROUP = N_HEADS // N_KV_HEADS
WINDOW = 128
BLOCK = 128
ROPE_THETA = 10000.0
Q_W = N_HEADS * HEAD_DIM
KV_W = N_KV_HEADS * HEAD_DIM
MIX_W = Q_W
CONF_K = 31
SC_K = 3
N_BRANCH = 3
D_FF = 2816
EPS = 1e-6
PAST_LEN = 16384

_O_Q = 0
_O_K = _O_Q + Q_W
_O_V = _O_K + KV_W
_O_CA = _O_V + KV_W
_O_CG = _O_CA + MIX_W
_O_SB = _O_CG + MIX_W
_O_SC = _O_SB + MIX_W
_O_SX = _O_SC + MIX_W
_O_GL = _O_SX + MIX_W
IN_W = _O_GL + N_BRANCH * D_MODEL

SUBLANES = 8
LANES = 128
VMEM_LIMIT_BYTES = 56 * 1024 * 1024

PROMPT_TILE = 512
SAMPLE_BATCH_TILE = 32
SAMPLE_UNROLL = 4
FFN_TILE = 1024
FFN_CHUNK = 256
CONF_HIST = 32
SC_HIST = 8

_BF16 = jnp.bfloat16
_F32 = jnp.float32


def _dot(a, b):
    return jnp.dot(a, b, preferred_element_type=_F32)


def _rmsnorm(x, g):
    return x * lax.rsqrt(jnp.mean(x * x, axis=-1, keepdims=True) + EPS) * g


def _rope(x, cos, sin_signed):
    lane = lax.broadcasted_iota(jnp.int32, (x.shape[0], LANES), 1)
    first_half = (lane % HEAD_DIM) < (HEAD_DIM // 2)
    cols = []
    for c in range(x.shape[1] // LANES):
        xb = x[:, c * LANES:(c + 1) * LANES]
        partner = jnp.where(first_half,
                            pltpu.roll(xb, LANES - HEAD_DIM // 2, 1),
                            pltpu.roll(xb, HEAD_DIM // 2, 1))
        cols.append(xb * cos + partner * sin_signed)
    return cols[0] if len(cols) == 1 else jnp.concatenate(cols, axis=-1)


def _sink_attention(q, k, v, mask, sink_col):
    s = lax.dot_general(q, k, (((1,), (1,)), ((), ())), preferred_element_type=_F32)
    s = jnp.where(mask, s, -jnp.inf)
    m = jnp.maximum(jnp.max(s, axis=-1, keepdims=True), sink_col)
    p = jnp.exp(s - m)
    denom = jnp.sum(p, axis=-1, keepdims=True) + jnp.exp(sink_col - m)
    return _dot(p.astype(_BF16), v) / denom


def _layernorm(x, g, b):
    mu = jnp.mean(x, axis=-1, keepdims=True)
    xc = x - mu
    var = jnp.mean(xc * xc, axis=-1, keepdims=True)
    return xc * lax.rsqrt(var + EPS) * g + b


def _conf_conv_rows(ext_ref, w_ref, r0, rows):
    base = CONF_HIST - (CONF_K - 1)
    out = None
    for r in range(SUBLANES):
        g = None
        for a in range((base + CONF_K - 1) // SUBLANES + 1):
            j = a * SUBLANES + r - base
            if j < 0 or j >= CONF_K:
                continue
            lo = r0 + a * SUBLANES
            term = w_ref[j:j + 1, :] * ext_ref[lo:lo + rows + SUBLANES, :]
            g = term if g is None else g + term
        if g is None:
            continue
        part = g[r:r + rows, :]
        out = part if out is None else out + part
    return out


def _branch_merge(h, win_ref, wb_ref, wout_ref, attn_o, conf_o, sc_o):
    merged = None
    for i, br in enumerate((attn_o, conf_o, sc_o)):
        gate = jax.nn.sigmoid(_dot(h, win_ref[:, _O_GL + i * D_MODEL:_O_GL + (i + 1) * D_MODEL]))
        term = gate * _dot(br.astype(_BF16), wb_ref[i])
        merged = term if merged is None else merged + term
    return _dot(merged.astype(_BF16), wout_ref[...])


def _prompt_mixer_kernel(sinks_ref, x_ref, cos_ref, sin_ref, n1_ref, win_ref, dww_ref, dwb_ref,
                         lng_ref, lnb_ref, scw_ref, wb_ref, wout_ref,
                         xo_ref, kst_ref, vst_ref, cst_ref, sst_ref,
                         kprev_ref, vprev_ref, cext_ref, sext_ref):
    t = PROMPT_TILE
    j = pl.program_id(1)

    @pl.when(j == 0)
    def _():
        kprev_ref[...] = jnp.zeros_like(kprev_ref)
        vprev_ref[...] = jnp.zeros_like(vprev_ref)
        cext_ref[...] = jnp.zeros_like(cext_ref)
        sext_ref[...] = jnp.zeros_like(sext_ref)

    x = x_ref[0]
    h = _rmsnorm(x, n1_ref[...]).astype(_BF16)

    qkv = _dot(h, win_ref[:, _O_Q:_O_CA])
    cacg = _dot(h, win_ref[:, _O_CA:_O_SB])

    cos = cos_ref[...]
    sin = sin_ref[...]
    q = _rope(qkv[:, _O_Q:_O_K], cos, sin) * (HEAD_DIM ** -0.5)
    k = _rope(qkv[:, _O_K:_O_V], cos, sin)
    v = qkv[:, _O_V:_O_CA]
    kst_ref[0] = k[t - WINDOW:, :]
    vst_ref[0] = v[t - WINDOW:, :]
    qb = q.astype(_BF16)
    kb = jnp.concatenate([kprev_ref[...], k.astype(_BF16)], axis=0)
    vb = jnp.concatenate([vprev_ref[...], v.astype(_BF16)], axis=0)
    kprev_ref[...] = kb[t:, :]
    vprev_ref[...] = vb[t:, :]

    qi = lax.broadcasted_iota(jnp.int32, (BLOCK, 2 * BLOCK), 0)
    kj = lax.broadcasted_iota(jnp.int32, (BLOCK, 2 * BLOCK), 1)
    diff = qi + BLOCK - kj
    band = (diff >= 0) & (diff < WINDOW)
    band4 = jnp.concatenate([band] * GROUP, axis=0)
    first_key = jnp.where(j == 0, BLOCK, 0)
    first_mask = band4 & jnp.concatenate([kj >= first_key] * GROUP, axis=0)

    sink_cols = [jnp.concatenate(
        [jnp.full((BLOCK, 1), sinks_ref[kh * GROUP + g], _F32) for g in range(GROUP)], axis=0)
        for kh in range(N_KV_HEADS)]

    def scores(blk):
        r0 = blk * BLOCK
        out = []
        for kh in range(N_KV_HEADS):
            qs = jnp.concatenate(
                [qb[r0:r0 + BLOCK, (kh * GROUP + g) * HEAD_DIM:(kh * GROUP + g + 1) * HEAD_DIM]
                 for g in range(GROUP)], axis=0)
            out.append(lax.dot_general(qs, kb[r0:r0 + 2 * BLOCK, kh * HEAD_DIM:(kh + 1) * HEAD_DIM],
                                       (((1,), (1,)), ((), ())), preferred_element_type=_F32))
        return out

    def softmax(blk, s_list):
        mask = first_mask if blk == 0 else band4
        out = []
        for kh in range(N_KV_HEADS):
            s = jnp.where(mask, s_list[kh], -jnp.inf)
            m = jnp.maximum(jnp.max(s, axis=-1, keepdims=True), sink_cols[kh])
            p = jnp.exp(s - m)
            denom = jnp.sum(p, axis=-1, keepdims=True) + jnp.exp(sink_cols[kh] - m)
            out.append((p.astype(_BF16), denom))
        return out

    def weighted_values(blk, p_list):
        r0 = blk * BLOCK
        heads = [None] * N_HEADS
        for kh in range(N_KV_HEADS):
            p, denom = p_list[kh]
            o = _dot(p, vb[r0:r0 + 2 * BLOCK, kh * HEAD_DIM:(kh + 1) * HEAD_DIM]) / denom
            for g in range(GROUP):
                heads[kh * GROUP + g] = o[g * BLOCK:(g + 1) * BLOCK, :]
        return jnp.concatenate(heads, axis=-1)

    n_blk = t // BLOCK

    u = cacg[:, :MIX_W] * jax.nn.sigmoid(cacg[:, MIX_W:])
    cext_ref[CONF_HIST:CONF_HIST + t, :] = u
    cst_ref[0] = u[t - CONF_HIST:, :]
    dw = _conf_conv_rows(cext_ref, dww_ref, 0, t)
    cext_ref[0:CONF_HIST, :] = u[t - CONF_HIST:, :]
    conf_o = jax.nn.silu(_layernorm(dw + dwb_ref[...], lng_ref[...], lnb_ref[...]))

    wide = [(_O_SC, _O_GL), (_O_GL, _O_GL + D_MODEL), (_O_GL + D_MODEL, _O_GL + 2 * D_MODEL),
            (_O_GL + 2 * D_MODEL, _O_GL + 3 * D_MODEL), (_O_SB, _O_SC)]
    wide_out = []
    attn_rows = []
    s_cur = scores(0)
    for blk in range(n_blk):
        if len(wide_out) < len(wide):
            lo, hi = wide[len(wide_out)]
            wide_out.append(_dot(h, win_ref[:, lo:hi]))
        p_cur = softmax(blk, s_cur)
        if blk + 1 < n_blk:
            s_cur = scores(blk + 1)
        attn_rows.append(weighted_values(blk, p_cur))
    while len(wide_out) < len(wide):
        lo, hi = wide[len(wide_out)]
        wide_out.append(_dot(h, win_ref[:, lo:hi]))
    scsx, g0, g1, g2, sb = wide_out
    attn_o = attn_rows[0] if len(attn_rows) == 1 else jnp.concatenate(attn_rows, axis=0)

    cu = scsx[:, :MIX_W] * scsx[:, MIX_W:]
    sext_ref[SC_HIST:SC_HIST + t, :] = cu
    ext = sext_ref[...]
    conv3 = None
    for jj in range(SC_K):
        off = SC_HIST - (SC_K - 1) + jj
        term = scw_ref[jj:jj + 1, :] * ext[off:off + t, :]
        conv3 = term if conv3 is None else conv3 + term
    sst_ref[0] = cu[t - SC_HIST:, :]
    sext_ref[0:SC_HIST, :] = cu[t - SC_HIST:, :]
    sc_o = sb * conv3

    m_sc = jax.nn.sigmoid(g2) * _dot(sc_o.astype(_BF16), wb_ref[2])
    m_attn = jax.nn.sigmoid(g0) * _dot(attn_o.astype(_BF16), wb_ref[0])
    m_conf = jax.nn.sigmoid(g1) * _dot(conf_o.astype(_BF16), wb_ref[1])
    merged = m_attn + m_conf + m_sc
    xo_ref[0] = x + _dot(merged.astype(_BF16), wout_ref[...])


def _sample_mixer_kernel(*refs, n_carried):
    (sinks_ref, x_ref, cos_ref, sin_ref, ck_ref, cv_ref, sconf_ref, ssc_ref,
     n1_ref, win_ref, dww_ref, dwb_ref, lng_ref, lnb_ref, scw_ref, wb_ref, wout_ref) = refs[:17]
    (xo_ref, ko_ref, vo_ref, co_ref, so_ref,
     q_s, k_s, v_s, u_s, cu_s, attn_s, dw_s, c3_s, cext_s, sext_s) = refs[17 + n_carried:]
    bc = SAMPLE_BATCH_TILE
    ts = x_ref.shape[0] // bc
    x = x_ref[...]
    h = _rmsnorm(x, n1_ref[...]).astype(_BF16)
    cos = cos_ref[...]
    sin = sin_ref[...]
    qkv = _dot(h, win_ref[:, _O_Q:_O_CA])
    cacg = _dot(h, win_ref[:, _O_CA:_O_SB])
    scsx = _dot(h, win_ref[:, _O_SC:_O_GL])
    q_s[...] = _rope(qkv[:, _O_Q:_O_K], cos, sin) * (HEAD_DIM ** -0.5)
    k_s[...] = _rope(qkv[:, _O_K:_O_V], cos, sin)
    v_s[...] = qkv[:, _O_V:_O_CA]
    u_s[...] = cacg[:, :MIX_W] * jax.nn.sigmoid(cacg[:, MIX_W:])
    cu_s[...] = scsx[:, :MIX_W] * scsx[:, MIX_W:]

    cache_len = ck_ref.shape[2]
    tok = lax.broadcasted_iota(jnp.int32, (GROUP * ts, 2 * cache_len), 0) % ts
    lane2 = lax.broadcasted_iota(jnp.int32, (GROUP * ts, 2 * cache_len), 1)
    new_tok = lane2 - (2 * cache_len - ts)
    mask = ((lane2 < ts) & (lane2 > tok)) | ((lane2 >= cache_len) & (new_tok <= tok))
    new_lanes = lax.broadcasted_iota(jnp.int32, (KV_W, cache_len), 1) >= cache_len - ts
    pad_rows = jnp.zeros((cache_len - ts, KV_W), _F32)
    sink_cols = [jnp.concatenate(
        [jnp.full((ts, 1), sinks_ref[kh * GROUP + g], _F32) for g in range(GROUP)], axis=0)
        for kh in range(N_KV_HEADS)]

    def attention(bs_):
        rows = [pl.ds(pl.multiple_of(b * ts, ts), ts) for b in bs_]
        vbs, scores = [], []
        def shifted_in(old_t, new_rows):
            new_t = jnp.concatenate([pad_rows, new_rows], axis=0).T
            return jnp.where(new_lanes, new_t, pltpu.roll(old_t, cache_len - ts, 1))

        for b, rw in zip(bs_, rows):
            k_old = ck_ref[b]
            v_old = cv_ref[b]
            k_out = shifted_in(k_old, k_s[rw, :])
            v_out = shifted_in(v_old, v_s[rw, :])
            ko_ref[b] = k_out
            vo_ref[b] = v_out
            kb = jnp.concatenate([k_old, k_out], axis=1).astype(_BF16)
            vbs.append(jnp.concatenate([v_old, v_out], axis=1).astype(_BF16))
            qrow = q_s[rw, :].astype(_BF16)
            for kh in range(N_KV_HEADS):
                qs = jnp.concatenate(
                    [qrow[:, (kh * GROUP + g) * HEAD_DIM:(kh * GROUP + g + 1) * HEAD_DIM]
                     for g in range(GROUP)], axis=0)
                scores.append(_dot(qs, kb[kh * HEAD_DIM:(kh + 1) * HEAD_DIM, :]))
        probs = []
        for i, s in enumerate(scores):
            sink_col = sink_cols[i % N_KV_HEADS]
            s = jnp.where(mask, s, -jnp.inf)
            m = jnp.maximum(jnp.max(s, axis=-1, keepdims=True), sink_col)
            p = jnp.exp(s - m)
            probs.append((p.astype(_BF16), jnp.sum(p, axis=-1, keepdims=True) + jnp.exp(sink_col - m)))
        for n, rw in enumerate(rows):
            heads = [None] * N_HEADS
            for kh in range(N_KV_HEADS):
                p, denom = probs[n * N_KV_HEADS + kh]
                o = lax.dot_general(p, vbs[n][kh * HEAD_DIM:(kh + 1) * HEAD_DIM, :],
                                    (((1,), (1,)), ((), ())), preferred_element_type=_F32) / denom
                for g in range(GROUP):
                    heads[kh * GROUP + g] = o[g * ts:(g + 1) * ts, :]
            attn_s[rw, :] = jnp.concatenate(heads, axis=-1)

    def convs(b, slot):
        rows = pl.ds(pl.multiple_of(b * ts, ts), ts)
        cext = cext_s.at[slot]
        cext[0:CONF_K - 1, :] = sconf_ref[b]
        cext[CONF_K - 1:CONF_K - 1 + ts, :] = u_s[rows, :]
        dw = None
        for jj in range(CONF_K):
            term = dww_ref[jj:jj + 1, :] * cext[jj:jj + ts, :]
            dw = term if dw is None else dw + term
        dw_s[rows, :] = dw
        co_ref[b] = cext[ts:ts + CONF_K - 1, :]

        sext = sext_s.at[slot]
        sext[0:SC_K - 1, :] = ssc_ref[b]
        sext[SC_K - 1:SC_K - 1 + ts, :] = cu_s[rows, :]
        c3 = None
        for jj in range(SC_K):
            term = scw_ref[jj:jj + 1, :] * sext[jj:jj + ts, :]
            c3 = term if c3 is None else c3 + term
        c3_s[rows, :] = c3
        so_ref[b] = sext[ts:ts + SC_K - 1, :]

    def body(i, carry):
        group = [i * SAMPLE_UNROLL + slot for slot in range(SAMPLE_UNROLL)]
        attention(group)
        for slot, b in enumerate(group):
            convs(b, slot)
        return carry

    lax.fori_loop(0, bc // SAMPLE_UNROLL, body, 0)

    conf_o = jax.nn.silu(_layernorm(dw_s[...] + dwb_ref[...], lng_ref[...], lnb_ref[...]))
    sc_o = _dot(h, win_ref[:, _O_SB:_O_SC]) * c3_s[...]
    xo_ref[...] = x + _branch_merge(h, win_ref, wb_ref, wout_ref, attn_s[...], conf_o, sc_o)


def _ffn_kernel(x_ref, n2_ref, w1_ref, w2_ref, fn_ref, o_ref, *, final):
    x = x_ref[...]
    hn = _rmsnorm(x, n2_ref[...]).astype(_BF16)

    def gate_up(c):
        lo = c * FFN_CHUNK
        return (_dot(hn, w1_ref[:, lo:lo + FFN_CHUNK]),
                _dot(hn, w1_ref[:, D_FF + lo:D_FF + lo + FFN_CHUNK]))

    acc = x
    n_chunks = D_FF // FFN_CHUNK
    g, up = gate_up(0)
    for c in range(n_chunks):
        act = (jax.nn.silu(g) * up).astype(_BF16)
        if c + 1 < n_chunks:
            g, up = gate_up(c + 1)
        acc = acc + _dot(act, w2_ref[c * FFN_CHUNK:(c + 1) * FFN_CHUNK, :])
    if final:
        acc = _rmsnorm(acc, fn_ref[...])
    o_ref[...] = acc


def _resident(shape):
    nd = len(shape)
    return pl.BlockSpec(shape, lambda *_: (0,) * nd, pipeline_mode=pl.Buffered(1))


def _smem():
    return pl.BlockSpec(memory_space=pltpu.SMEM)


def _params(semantics):
    return pltpu.CompilerParams(dimension_semantics=semantics, vmem_limit_bytes=VMEM_LIMIT_BYTES)


def _layer_weight_specs():
    return [
        _resident((1, D_MODEL)),
        _resident((D_MODEL, IN_W)),
        _resident((CONF_K, MIX_W)),
        _resident((1, MIX_W)),
        _resident((1, MIX_W)),
        _resident((1, MIX_W)),
        _resident((SC_K, MIX_W)),
        _resident((N_BRANCH, MIX_W, D_MODEL)),
        _resident((D_MODEL, D_MODEL)),
    ]


def _prompt_mixer(x, cos, sin, sinks, lw):
    b, s, _ = x.shape
    t = PROMPT_TILE
    grid = (b, s // t)
    return pl.pallas_call(
        _prompt_mixer_kernel,
        grid=grid,
        in_specs=[_smem(),
                  pl.BlockSpec((1, t, D_MODEL), lambda i, j: (i, j, 0)),
                  pl.BlockSpec((t, LANES), lambda i, j: (j, 0)),
                  pl.BlockSpec((t, LANES), lambda i, j: (j, 0))] + _layer_weight_specs(),
        out_specs=[pl.BlockSpec((1, t, D_MODEL), lambda i, j: (i, j, 0)),
                   pl.BlockSpec((1, WINDOW, KV_W), lambda i, j: (i, 0, 0)),
                   pl.BlockSpec((1, WINDOW, KV_W), lambda i, j: (i, 0, 0)),
                   pl.BlockSpec((1, CONF_HIST, MIX_W), lambda i, j: (i, 0, 0)),
                   pl.BlockSpec((1, SC_HIST, MIX_W), lambda i, j: (i, 0, 0))],
        out_shape=[jax.ShapeDtypeStruct((b, s, D_MODEL), _F32),
                   jax.ShapeDtypeStruct((b, WINDOW, KV_W), _F32),
                   jax.ShapeDtypeStruct((b, WINDOW, KV_W), _F32),
                   jax.ShapeDtypeStruct((b, CONF_HIST, MIX_W), _F32),
                   jax.ShapeDtypeStruct((b, SC_HIST, MIX_W), _F32)],
        scratch_shapes=[pltpu.VMEM((BLOCK, KV_W), _BF16),
                        pltpu.VMEM((BLOCK, KV_W), _BF16),
                        pltpu.VMEM((CONF_HIST + t + SUBLANES, MIX_W), _F32),
                        pltpu.VMEM((SC_HIST + t, MIX_W), _F32)],
        compiler_params=_params(("arbitrary", "arbitrary")),
        name="prompt_mixer",
    )(sinks, x, cos, sin, *lw)


def _sample_mixer(layer, x, cos, sin, ck, cv, sconf, ssc, sinks, lw, carried):
    n_tok = x.shape[0]
    bc = SAMPLE_BATCH_TILE
    nb, cache_len = ck.shape[1], ck.shape[3]
    ts = n_tok // nb
    r = bc * ts
    rows = lambda i: (i, 0)
    seqs = lambda i: (layer, i, 0, 0)
    state_specs = [pl.BlockSpec((None, bc, KV_W, cache_len), seqs),
                   pl.BlockSpec((None, bc, KV_W, cache_len), seqs),
                   pl.BlockSpec((None, bc, CONF_K - 1, MIX_W), seqs),
                   pl.BlockSpec((None, bc, SC_K - 1, MIX_W), seqs)]
    n_fixed_in = 8 + len(lw)
    return pl.pallas_call(
        functools.partial(_sample_mixer_kernel, n_carried=len(carried)),
        grid=(nb // bc,),
        in_specs=[_smem(),
                  pl.BlockSpec((r, D_MODEL), rows),
                  _resident((r, LANES)),
                  _resident((r, LANES))] + state_specs + _layer_weight_specs()
                 + [pl.BlockSpec(memory_space=pl.ANY)] * len(carried),
        out_specs=[pl.BlockSpec((r, D_MODEL), rows)] + state_specs,
        out_shape=[jax.ShapeDtypeStruct((n_tok, D_MODEL), _F32),
                   jax.ShapeDtypeStruct(ck.shape, _F32),
                   jax.ShapeDtypeStruct(cv.shape, _F32),
                   jax.ShapeDtypeStruct(sconf.shape, _F32),
                   jax.ShapeDtypeStruct(ssc.shape, _F32)],
        input_output_aliases={n_fixed_in + i: 1 + i for i in range(len(carried))},
        scratch_shapes=[pltpu.VMEM((r, Q_W), _F32),
                        pltpu.VMEM((r, KV_W), _F32),
                        pltpu.VMEM((r, KV_W), _F32),
                        pltpu.VMEM((r, MIX_W), _F32),
                        pltpu.VMEM((r, MIX_W), _F32),
                        pltpu.VMEM((r, Q_W), _F32),
                        pltpu.VMEM((r, MIX_W), _F32),
                        pltpu.VMEM((r, MIX_W), _F32),
                        pltpu.VMEM((SAMPLE_UNROLL, CONF_K - 1 + ts, MIX_W), _F32),
                        pltpu.VMEM((SAMPLE_UNROLL, SC_K - 1 + ts, MIX_W), _F32)],
        compiler_params=_params(("arbitrary",)),
        name="sample_mixer",
    )(sinks, x, cos, sin, ck, cv, sconf, ssc, *lw, *carried)


def _ffn(x, n2, w1, w2, fn, final, name):
    n_tok = x.shape[0]
    t = FFN_TILE
    rows = lambda i: (i, 0)
    return pl.pallas_call(
        functools.partial(_ffn_kernel, final=final),
        grid=(n_tok // t,),
        in_specs=[pl.BlockSpec((t, D_MODEL), rows),
                  _resident((1, D_MODEL)),
                  _resident((D_MODEL, 2 * D_FF)),
                  _resident((D_FF, D_MODEL)),
                  _resident((1, D_MODEL))],
        out_specs=pl.BlockSpec((t, D_MODEL), rows),
        out_shape=jax.ShapeDtypeStruct((n_tok, D_MODEL), _F32),
        compiler_params=_params(("arbitrary",)),
        name=name,
    )(x, n2, w1, w2, fn)


def _rope_tables(positions):
    half = HEAD_DIM // 2
    inv = ROPE_THETA ** (-np.arange(half, dtype=np.float64) / half)
    ang = np.asarray(positions, np.float64)[:, None] * inv[None, :]
    cos = np.cos(ang)
    sin = np.sin(ang)
    cos_t = np.tile(np.concatenate([cos, cos], axis=-1), (1, LANES // HEAD_DIM))
    sin_t = np.tile(np.concatenate([-sin, sin], axis=-1), (1, LANES // HEAD_DIM))
    return jnp.asarray(cos_t, _F32), jnp.asarray(sin_t, _F32)


def kernel(x_prompt, x_sample, cache_k, cache_v, state_conf, state_sconv, norm1, w_in, sinks,
           conf_dw_w, conf_dw_b, conf_ln_g, conf_ln_b, sconv_w, w_branch, w_out, norm2,
           w_ffn_in, w_ffn_out, final_norm):
    depth = w_in.shape[0]
    bp, sp, _ = x_prompt.shape
    bs, ts, _ = x_sample.shape
    cache_len = cache_k.shape[2]
    assert sp % PROMPT_TILE == 0 and PROMPT_TILE % BLOCK == 0 and sp >= WINDOW
    assert bs % SAMPLE_BATCH_TILE == 0 and SAMPLE_BATCH_TILE % SAMPLE_UNROLL == 0
    assert ts % SUBLANES == 0 and cache_len == WINDOW
    assert (bp * sp) % FFN_TILE == 0 and (bs * ts) % FFN_TILE == 0 and D_FF % FFN_CHUNK == 0

    cos_p, sin_p = _rope_tables(np.arange(sp))
    cos_s, sin_s = _rope_tables(np.tile(PAST_LEN + np.arange(ts), SAMPLE_BATCH_TILE))

    xp = x_prompt
    xs = x_sample.reshape(bs * ts, D_MODEL)
    ck = cache_k.transpose(0, 1, 3, 4, 2).reshape(depth, bs, KV_W, cache_len)
    cv = cache_v.transpose(0, 1, 3, 4, 2).reshape(depth, bs, KV_W, cache_len)
    fn = final_norm.reshape(1, D_MODEL)
    prompt_states = [[] for _ in range(4)]
    sample_states = tuple(jnp.zeros(a.shape, _F32) for a in (ck, cv, state_conf, state_sconv))
    for l in range(depth):
        lw = (norm1[l].reshape(1, D_MODEL), w_in[l].astype(_BF16), conf_dw_w[l],
              conf_dw_b[l].reshape(1, MIX_W), conf_ln_g[l].reshape(1, MIX_W),
              conf_ln_b[l].reshape(1, MIX_W), sconv_w[l], w_branch[l].astype(_BF16),
              w_out[l].astype(_BF16))
        n2 = norm2[l].reshape(1, D_MODEL)
        w1 = w_ffn_in[l].astype(_BF16)
        w2 = w_ffn_out[l].astype(_BF16)
        final = l == depth - 1

        xp, kp, vp, cp, scp = _prompt_mixer(xp, cos_p, sin_p, sinks[l], lw)
        xp = _ffn(xp.reshape(bp * sp, D_MODEL), n2, w1, w2, fn, final, "prompt_ffn").reshape(bp, sp, D_MODEL)
        xs, *sample_states = _sample_mixer(l, xs, cos_s, sin_s, ck, cv, state_conf, state_sconv,
                                           sinks[l], lw, tuple(sample_states))
        xs = _ffn(xs, n2, w1, w2, fn, final, "sample_ffn")

        prompt_states[0].append(kp.reshape(bp, WINDOW, N_KV_HEADS, HEAD_DIM))
        prompt_states[1].append(vp.reshape(bp, WINDOW, N_KV_HEADS, HEAD_DIM))
        prompt_states[2].append(cp[:, CONF_HIST - (CONF_K - 1):, :])
        prompt_states[3].append(scp[:, SC_HIST - (SC_K - 1):, :])
    k_s, v_s, c_s, s_s = sample_states
    kv_t_shape = (depth, bs, N_KV_HEADS, HEAD_DIM, cache_len)
    k_s = k_s.reshape(kv_t_shape).transpose(0, 1, 4, 2, 3)
    v_s = v_s.reshape(kv_t_shape).transpose(0, 1, 4, 2, 3)
    return ((xp, xs.reshape(bs, ts, D_MODEL)) + tuple(jnp.stack(o) for o in prompt_states)
            + (k_s, v_s, c_s, s_s))
```

```python
import functools

import numpy as np
import jax
import jax.numpy as jnp
from jax import lax
from jax.experimental import pallas as pl
from jax.experimental.pallas import tpu as pltpu

D_MODEL = 1024
HEAD_DIM = 64
N_HEADS = 8
N_KV_HEADS = 2
GROUP = N_HEADS // N_KV_HEADS
WINDOW = 128
BLOCK = 128
ROPE_THETA = 10000.0
Q_W = N_HEADS * HEAD_DIM
KV_W = N_KV_HEADS * HEAD_DIM
MIX_W = Q_W
CONF_K = 31
SC_K = 3
N_BRANCH = 3
D_FF = 2816
EPS = 1e-6
PAST_LEN = 16384

_O_Q = 0
_O_K = _O_Q + Q_W
_O_V = _O_K + KV_W
_O_CA = _O_V + KV_W
_O_CG = _O_CA + MIX_W
_O_SB = _O_CG + MIX_W
_O_SC = _O_SB + MIX_W
_O_SX = _O_SC + MIX_W
_O_GL = _O_SX + MIX_W
IN_W = _O_GL + N_BRANCH * D_MODEL

SUBLANES = 8
LANES = 128
VMEM_LIMIT_BYTES = 56 * 1024 * 1024

PROMPT_TILE = 512
SAMPLE_BATCH_TILE = 32
SAMPLE_UNROLL = 4
FFN_TILE = 1024
FFN_CHUNK = 256
CONF_HIST = 32
SC_HIST = 8

_BF16 = jnp.bfloat16
_F32 = jnp.float32


def _dot(a, b):
    return jnp.dot(a, b, preferred_element_type=_F32)


def _rmsnorm(x, g):
    return x * lax.rsqrt(jnp.mean(x * x, axis=-1, keepdims=True) + EPS) * g


def _rope(x, cos, sin_signed):
    lane = lax.broadcasted_iota(jnp.int32, (x.shape[0], LANES), 1)
    first_half = (lane % HEAD_DIM) < (HEAD_DIM // 2)
    cols = []
    for c in range(x.shape[1] // LANES):
        xb = x[:, c * LANES:(c + 1) * LANES]
        partner = jnp.where(first_half,
                            pltpu.roll(xb, LANES - HEAD_DIM // 2, 1),
                            pltpu.roll(xb, HEAD_DIM // 2, 1))
        cols.append(xb * cos + partner * sin_signed)
    return cols[0] if len(cols) == 1 else jnp.concatenate(cols, axis=-1)


def _sink_attention(q, k, v, mask, sink_col):
    s = lax.dot_general(q, k, (((1,), (1,)), ((), ())), preferred_element_type=_F32)
    s = jnp.where(mask, s, -jnp.inf)
    m = jnp.maximum(jnp.max(s, axis=-1, keepdims=True), sink_col)
    p = jnp.exp(s - m)
    denom = jnp.sum(p, axis=-1, keepdims=True) + jnp.exp(sink_col - m)
    return _dot(p.astype(_BF16), v) / denom


def _layernorm(x, g, b):
    mu = jnp.mean(x, axis=-1, keepdims=True)
    xc = x - mu
    var = jnp.mean(xc * xc, axis=-1, keepdims=True)
    return xc * lax.rsqrt(var + EPS) * g + b


def _conf_conv_rows(ext_ref, w_ref, r0, rows):
    base = CONF_HIST - (CONF_K - 1)
    out = None
    for r in range(SUBLANES):
        g = None
        for a in range((base + CONF_K - 1) // SUBLANES + 1):
            j = a * SUBLANES + r - base
            if j < 0 or j >= CONF_K:
                continue
            lo = r0 + a * SUBLANES
            term = w_ref[j:j + 1, :] * ext_ref[lo:lo + rows + SUBLANES, :]
            g = term if g is None else g + term
        if g is None:
            continue
        part = g[r:r + rows, :]
        out = part if out is None else out + part
    return out


def _branch_merge(h, win_ref, wb_ref, wout_ref, attn_o, conf_o, sc_o):
    merged = None
    for i, br in enumerate((attn_o, conf_o, sc_o)):
        gate = jax.nn.sigmoid(_dot(h, win_ref[:, _O_GL + i * D_MODEL:_O_GL + (i + 1) * D_MODEL]))
        term = gate * _dot(br.astype(_BF16), wb_ref[i])
        merged = term if merged is None else merged + term
    return _dot(merged.astype(_BF16), wout_ref[...])


def _prompt_mixer_kernel(sinks_ref, x_ref, cos_ref, sin_ref, n1_ref, win_ref, dww_ref, dwb_ref,
                         lng_ref, lnb_ref, scw_ref, wb_ref, wout_ref,
                         xo_ref, kst_ref, vst_ref, cst_ref, sst_ref,
                         kprev_ref, vprev_ref, cext_ref, sext_ref):
    t = PROMPT_TILE
    j = pl.program_id(1)

    @pl.when(j == 0)
    def _():
        kprev_ref[...] = jnp.zeros_like(kprev_ref)
        vprev_ref[...] = jnp.zeros_like(vprev_ref)
        cext_ref[...] = jnp.zeros_like(cext_ref)
        sext_ref[...] = jnp.zeros_like(sext_ref)

    x = x_ref[0]
    h = _rmsnorm(x, n1_ref[...]).astype(_BF16)

    qkv = _dot(h, win_ref[:, _O_Q:_O_CA])
    cacg = _dot(h, win_ref[:, _O_CA:_O_SB])

    cos = cos_ref[...]
    sin = sin_ref[...]
    q = _rope(qkv[:, _O_Q:_O_K], cos, sin) * (HEAD_DIM ** -0.5)
    k = _rope(qkv[:, _O_K:_O_V], cos, sin)
    v = qkv[:, _O_V:_O_CA]
    kst_ref[0] = k[t - WINDOW:, :]
    vst_ref[0] = v[t - WINDOW:, :]
    qb = q.astype(_BF16)
    kb = jnp.concatenate([kprev_ref[...], k.astype(_BF16)], axis=0)
    vb = jnp.concatenate([vprev_ref[...], v.astype(_BF16)], axis=0)
    kprev_ref[...] = kb[t:, :]
    vprev_ref[...] = vb[t:, :]

    qi = lax.broadcasted_iota(jnp.int32, (BLOCK, 2 * BLOCK), 0)
    kj = lax.broadcasted_iota(jnp.int32, (BLOCK, 2 * BLOCK), 1)
    diff = qi + BLOCK - kj
    band = (diff >= 0) & (diff < WINDOW)
    band4 = jnp.concatenate([band] * GROUP, axis=0)
    first_key = jnp.where(j == 0, BLOCK, 0)
    first_mask = band4 & jnp.concatenate([kj >= first_key] * GROUP, axis=0)

    sink_cols = [jnp.concatenate(
        [jnp.full((BLOCK, 1), sinks_ref[kh * GROUP + g], _F32) for g in range(GROUP)], axis=0)
        for kh in range(N_KV_HEADS)]

    def scores(blk):
        r0 = blk * BLOCK
        out = []
        for kh in range(N_KV_HEADS):
            qs = jnp.concatenate(
                [qb[r0:r0 + BLOCK, (kh * GROUP + g) * HEAD_DIM:(kh * GROUP + g + 1) * HEAD_DIM]
                 for g in range(GROUP)], axis=0)
            out.append(lax.dot_general(qs, kb[r0:r0 + 2 * BLOCK, kh * HEAD_DIM:(kh + 1) * HEAD_DIM],
                                       (((1,), (1,)), ((), ())), preferred_element_type=_F32))
        return out

    def softmax(blk, s_list):
        mask = first_mask if blk == 0 else band4
        out = []
        for kh in range(N_KV_HEADS):
            s = jnp.where(mask, s_list[kh], -jnp.inf)
            m = jnp.maximum(jnp.max(s, axis=-1, keepdims=True), sink_cols[kh])
            p = jnp.exp(s - m)
            denom = jnp.sum(p, axis=-1, keepdims=True) + jnp.exp(sink_cols[kh] - m)
            out.append((p.astype(_BF16), denom))
        return out

    def weighted_values(blk, p_list):
        r0 = blk * BLOCK
        heads = [None] * N_HEADS
        for kh in range(N_KV_HEADS):
            p, denom = p_list[kh]
            o = _dot(p, vb[r0:r0 + 2 * BLOCK, kh * HEAD_DIM:(kh + 1) * HEAD_DIM]) / denom
            for g in range(GROUP):
                heads[kh * GROUP + g] = o[g * BLOCK:(g + 1) * BLOCK, :]
        return jnp.concatenate(heads, axis=-1)

    n_blk = t // BLOCK

    u = cacg[:, :MIX_W] * jax.nn.sigmoid(cacg[:, MIX_W:])
    cext_ref[CONF_HIST:CONF_HIST + t, :] = u
    cst_ref[0] = u[t - CONF_HIST:, :]
    dw = _conf_conv_rows(cext_ref, dww_ref, 0, t)
    cext_ref[0:CONF_HIST, :] = u[t - CONF_HIST:, :]
    conf_o = jax.nn.silu(_layernorm(dw + dwb_ref[...], lng_ref[...], lnb_ref[...]))

    wide = [(_O_SC, _O_GL), (_O_GL, _O_GL + D_MODEL), (_O_GL + D_MODEL, _O_GL + 2 * D_MODEL),
            (_O_GL + 2 * D_MODEL, _O_GL + 3 * D_MODEL), (_O_SB, _O_SC)]
    wide_out = []
    attn_rows = []
    s_cur = scores(0)
    for blk in range(n_blk):
        if len(wide_out) < len(wide):
            lo, hi = wide[len(wide_out)]
            wide_out.append(_dot(h, win_ref[:, lo:hi]))
        p_cur = softmax(blk, s_cur)
        if blk + 1 < n_blk:
            s_cur = scores(blk + 1)
        attn_rows.append(weighted_values(blk, p_cur))
    while len(wide_out) < len(wide):
        lo, hi = wide[len(wide_out)]
        wide_out.append(_dot(h, win_ref[:, lo:hi]))
    scsx, g0, g1, g2, sb = wide_out
    attn_o = attn_rows[0] if len(attn_rows) == 1 else jnp.concatenate(attn_rows, axis=0)

    cu = scsx[:, :MIX_W] * scsx[:, MIX_W:]
    sext_ref[SC_HIST:SC_HIST + t, :] = cu
    ext = sext_ref[...]
    conv3 = None
    for jj in range(SC_K):
        off = SC_HIST - (SC_K - 1) + jj
        term = scw_ref[jj:jj + 1, :] * ext[off:off + t, :]
        conv3 = term if conv3 is None else conv3 + term
    sst_ref[0] = cu[t - SC_HIST:, :]
    sext_ref[0:SC_HIST, :] = cu[t - SC_HIST:, :]
    sc_o = sb * conv3

    m_sc = jax.nn.sigmoid(g2) * _dot(sc_o.astype(_BF16), wb_ref[2])
    m_attn = jax.nn.sigmoid(g0) * _dot(attn_o.astype(_BF16), wb_ref[0])
    m_conf = jax.nn.sigmoid(g1) * _dot(conf_o.astype(_BF16), wb_ref[1])
    merged = m_attn + m_conf + m_sc
    xo_ref[0] = x + _dot(merged.astype(_BF16), wout_ref[...])


def _sample_mixer_kernel(*refs, n_carried):
    (sinks_ref, x_ref, cos_ref, sin_ref, ck_ref, cv_ref, sconf_ref, ssc_ref,
     n1_ref, win_ref, dww_ref, dwb_ref, lng_ref, lnb_ref, scw_ref, wb_ref, wout_ref) = refs[:17]
    (xo_ref, ko_ref, vo_ref, co_ref, so_ref,
     q_s, k_s, v_s, u_s, cu_s, attn_s, dw_s, c3_s, cext_s, sext_s) = refs[17 + n_carried:]
    bc = SAMPLE_BATCH_TILE
    ts = x_ref.shape[0] // bc
    x = x_ref[...]
    h = _rmsnorm(x, n1_ref[...]).astype(_BF16)
    cos = cos_ref[...]
    sin = sin_ref[...]
    qkv = _dot(h, win_ref[:, _O_Q:_O_CA])
    cacg = _dot(h, win_ref[:, _O_CA:_O_SB])
    scsx = _dot(h, win_ref[:, _O_SC:_O_GL])
    q_s[...] = _rope(qkv[:, _O_Q:_O_K], cos, sin) * (HEAD_DIM ** -0.5)
    k_s[...] = _rope(qkv[:, _O_K:_O_V], cos, sin)
    v_s[...] = qkv[:, _O_V:_O_CA]
    u_s[...] = cacg[:, :MIX_W] * jax.nn.sigmoid(cacg[:, MIX_W:])
    cu_s[...] = scsx[:, :MIX_W] * scsx[:, MIX_W:]

    cache_len = ck_ref.shape[2]
    tok = lax.broadcasted_iota(jnp.int32, (GROUP * ts, 2 * cache_len), 0) % ts
    lane2 = lax.broadcasted_iota(jnp.int32, (GROUP * ts, 2 * cache_len), 1)
    new_tok = lane2 - (2 * cache_len - ts)
    mask = ((lane2 < ts) & (lane2 > tok)) | ((lane2 >= cache_len) & (new_tok <= tok))
    new_lanes = lax.broadcasted_iota(jnp.int32, (KV_W, cache_len), 1) >= cache_len - ts
    pad_rows = jnp.zeros((cache_len - ts, KV_W), _F32)
    sink_cols = [jnp.concatenate(
        [jnp.full((ts, 1), sinks_ref[kh * GROUP + g], _F32) for g in range(GROUP)], axis=0)
        for kh in range(N_KV_HEADS)]

    def attention(bs_):
        rows = [pl.ds(pl.multiple_of(b * ts, ts), ts) for b in bs_]
        vbs, scores = [], []
        def shifted_in(old_t, new_rows):
            new_t = jnp.concatenate([pad_rows, new_rows], axis=0).T
            return jnp.where(new_lanes, new_t, pltpu.roll(old_t, cache_len - ts, 1))

        for b, rw in zip(bs_, rows):
            k_old = ck_ref[b]
            v_old = cv_ref[b]
            k_out = shifted_in(k_old, k_s[rw, :])
            v_out = shifted_in(v_old, v_s[rw, :])
            ko_ref[b] = k_out
            vo_ref[b] = v_out
            kb = jnp.concatenate([k_old, k_out], axis=1).astype(_BF16)
            vbs.append(jnp.concatenate([v_old, v_out], axis=1).astype(_BF16))
            qrow = q_s[rw, :].astype(_BF16)
            for kh in range(N_KV_HEADS):
                qs = jnp.concatenate(
                    [qrow[:, (kh * GROUP + g) * HEAD_DIM:(kh * GROUP + g + 1) * HEAD_DIM]
                     for g in range(GROUP)], axis=0)
                scores.append(_dot(qs, kb[kh * HEAD_DIM:(kh + 1) * HEAD_DIM, :]))
        probs = []
        for i, s in enumerate(scores):
            sink_col = sink_cols[i % N_KV_HEADS]
            s = jnp.where(mask, s, -jnp.inf)
            m = jnp.maximum(jnp.max(s, axis=-1, keepdims=True), sink_col)
            p = jnp.exp(s - m)
            probs.append((p.astype(_BF16), jnp.sum(p, axis=-1, keepdims=True) + jnp.exp(sink_col - m)))
        for n, rw in enumerate(rows):
            heads = [None] * N_HEADS
            for kh in range(N_KV_HEADS):
                p, denom = probs[n * N_KV_HEADS + kh]
                o = lax.dot_general(p, vbs[n][kh * HEAD_DIM:(kh + 1) * HEAD_DIM, :],
                                    (((1,), (1,)), ((), ())), preferred_element_type=_F32) / denom
                for g in range(GROUP):
                    heads[kh * GROUP + g] = o[g * ts:(g + 1) * ts, :]
            attn_s[rw, :] = jnp.concatenate(heads, axis=-1)

    def convs(b, slot):
        rows = pl.ds(pl.multiple_of(b * ts, ts), ts)
        cext = cext_s.at[slot]
        cext[0:CONF_K - 1, :] = sconf_ref[b]
        cext[CONF_K - 1:CONF_K - 1 + ts, :] = u_s[rows, :]
        dw = None
        for jj in range(CONF_K):
            term = dww_ref[jj:jj + 1, :] * cext[jj:jj + ts, :]
            dw = term if dw is None else dw + term
        dw_s[rows, :] = dw
        co_ref[b] = cext[ts:ts + CONF_K - 1, :]

        sext = sext_s.at[slot]
        sext[0:SC_K - 1, :] = ssc_ref[b]
        sext[SC_K - 1:SC_K - 1 + ts, :] = cu_s[rows, :]
        c3 = None
        for jj in range(SC_K):
            term = scw_ref[jj:jj + 1, :] * sext[jj:jj + ts, :]
            c3 = term if c3 is None else c3 + term
        c3_s[rows, :] = c3
        so_ref[b] = sext[ts:ts + SC_K - 1, :]

    def body(i, carry):
        group = [i * SAMPLE_UNROLL + slot for slot in range(SAMPLE_UNROLL)]
        attention(group)
        for slot, b in enumerate(group):
            convs(b, slot)
        return carry

    lax.fori_loop(0, bc // SAMPLE_UNROLL, body, 0)

    conf_o = jax.nn.silu(_layernorm(dw_s[...] + dwb_ref[...], lng_ref[...], lnb_ref[...]))
    sc_o = _dot(h, win_ref[:, _O_SB:_O_SC]) * c3_s[...]
    xo_ref[...] = x + _branch_merge(h, win_ref, wb_ref, wout_ref, attn_s[...], conf_o, sc_o)


def _ffn_kernel(x_ref, n2_ref, w1_ref, w2_ref, fn_ref, o_ref, *, final):
    x = x_ref[...]
    hn = _rmsnorm(x, n2_ref[...]).astype(_BF16)

    def gate_up(c):
        lo = c * FFN_CHUNK
        return (_dot(hn, w1_ref[:, lo:lo + FFN_CHUNK]),
                _dot(hn, w1_ref[:, D_FF + lo:D_FF + lo + FFN_CHUNK]))

    acc = x
    n_chunks = D_FF // FFN_CHUNK
    g, up = gate_up(0)
    for c in range(n_chunks):
        act = (jax.nn.silu(g) * up).astype(_BF16)
        if c + 1 < n_chunks:
            g, up = gate_up(c + 1)
        acc = acc + _dot(act, w2_ref[c * FFN_CHUNK:(c + 1) * FFN_CHUNK, :])
    if final:
        acc = _rmsnorm(acc, fn_ref[...])
    o_ref[...] = acc


def _resident(shape):
    nd = len(shape)
    return pl.BlockSpec(shape, lambda *_: (0,) * nd, pipeline_mode=pl.Buffered(1))


def _smem():
    return pl.BlockSpec(memory_space=pltpu.SMEM)


def _params(semantics):
    return pltpu.CompilerParams(dimension_semantics=semantics, vmem_limit_bytes=VMEM_LIMIT_BYTES)


def _layer_resident(shape, layer):
    nd = len(shape)
    return pl.BlockSpec((None,) + shape, lambda *_: (layer,) + (0,) * nd, pipeline_mode=pl.Buffered(1))


def _layer_weight_specs(layer):
    return [
        _layer_resident((1, D_MODEL), layer),
        _layer_resident((D_MODEL, IN_W), layer),
        _layer_resident((CONF_K, MIX_W), layer),
        _layer_resident((1, MIX_W), layer),
        _layer_resident((1, MIX_W), layer),
        _layer_resident((1, MIX_W), layer),
        _layer_resident((SC_K, MIX_W), layer),
        _layer_resident((N_BRANCH, MIX_W, D_MODEL), layer),
        _layer_resident((D_MODEL, D_MODEL), layer),
    ]


def _prompt_mixer(layer, x, cos, sin, sinks, lw):
    b, s, _ = x.shape
    t = PROMPT_TILE
    grid = (b, s // t)
    return pl.pallas_call(
        _prompt_mixer_kernel,
        grid=grid,
        in_specs=[_smem(),
                  pl.BlockSpec((1, t, D_MODEL), lambda i, j: (i, j, 0)),
                  pl.BlockSpec((t, LANES), lambda i, j: (j, 0)),
                  pl.BlockSpec((t, LANES), lambda i, j: (j, 0))] + _layer_weight_specs(layer),
        out_specs=[pl.BlockSpec((1, t, D_MODEL), lambda i, j: (i, j, 0)),
                   pl.BlockSpec((1, WINDOW, KV_W), lambda i, j: (i, 0, 0)),
                   pl.BlockSpec((1, WINDOW, KV_W), lambda i, j: (i, 0, 0)),
                   pl.BlockSpec((1, CONF_HIST, MIX_W), lambda i, j: (i, 0, 0)),
                   pl.BlockSpec((1, SC_HIST, MIX_W), lambda i, j: (i, 0, 0))],
        out_shape=[jax.ShapeDtypeStruct((b, s, D_MODEL), _F32),
                   jax.ShapeDtypeStruct((b, WINDOW, KV_W), _F32),
                   jax.ShapeDtypeStruct((b, WINDOW, KV_W), _F32),
                   jax.ShapeDtypeStruct((b, CONF_HIST, MIX_W), _F32),
                   jax.ShapeDtypeStruct((b, SC_HIST, MIX_W), _F32)],
        scratch_shapes=[pltpu.VMEM((BLOCK, KV_W), _BF16),
                        pltpu.VMEM((BLOCK, KV_W), _BF16),
                        pltpu.VMEM((CONF_HIST + t + SUBLANES, MIX_W), _F32),
                        pltpu.VMEM((SC_HIST + t, MIX_W), _F32)],
        compiler_params=_params(("arbitrary", "arbitrary")),
        name="prompt_mixer",
    )(sinks, x, cos, sin, *lw)


def _sample_mixer(layer, x, cos, sin, ck, cv, sconf, ssc, sinks, lw, carried):
    n_tok = x.shape[0]
    bc = SAMPLE_BATCH_TILE
    nb, cache_len = ck.shape[1], ck.shape[3]
    ts = n_tok // nb
    r = bc * ts
    rows = lambda i: (i, 0)
    seqs = lambda i: (layer, i, 0, 0)
    state_specs = [pl.BlockSpec((None, bc, KV_W, cache_len), seqs),
                   pl.BlockSpec((None, bc, KV_W, cache_len), seqs),
                   pl.BlockSpec((None, bc, CONF_K - 1, MIX_W), seqs),
                   pl.BlockSpec((None, bc, SC_K - 1, MIX_W), seqs)]
    n_fixed_in = 8 + len(lw)
    return pl.pallas_call(
        functools.partial(_sample_mixer_kernel, n_carried=len(carried)),
        grid=(nb // bc,),
        in_specs=[_smem(),
                  pl.BlockSpec((r, D_MODEL), rows),
                  _resident((r, LANES)),
                  _resident((r, LANES))] + state_specs + _layer_weight_specs(layer)
                 + [pl.BlockSpec(memory_space=pl.ANY)] * len(carried),
        out_specs=[pl.BlockSpec((r, D_MODEL), rows)] + state_specs,
        out_shape=[jax.ShapeDtypeStruct((n_tok, D_MODEL), _F32),
                   jax.ShapeDtypeStruct(ck.shape, _F32),
                   jax.ShapeDtypeStruct(cv.shape, _F32),
                   jax.ShapeDtypeStruct(sconf.shape, _F32),
                   jax.ShapeDtypeStruct(ssc.shape, _F32)],
        input_output_aliases={n_fixed_in + i: 1 + i for i in range(len(carried))},
        scratch_shapes=[pltpu.VMEM((r, Q_W), _F32),
                        pltpu.VMEM((r, KV_W), _F32),
                        pltpu.VMEM((r, KV_W), _F32),
                        pltpu.VMEM((r, MIX_W), _F32),
                        pltpu.VMEM((r, MIX_W), _F32),
                        pltpu.VMEM((r, Q_W), _F32),
                        pltpu.VMEM((r, MIX_W), _F32),
                        pltpu.VMEM((r, MIX_W), _F32),
                        pltpu.VMEM((SAMPLE_UNROLL, CONF_K - 1 + ts, MIX_W), _F32),
                        pltpu.VMEM((SAMPLE_UNROLL, SC_K - 1 + ts, MIX_W), _F32)],
        compiler_params=_params(("arbitrary",)),
        name="sample_mixer",
    )(sinks, x, cos, sin, ck, cv, sconf, ssc, *lw, *carried)


def _ffn(layer, x, n2, w1, w2, fn, final, name):
    n_tok = x.shape[0]
    t = FFN_TILE
    rows = lambda i: (i, 0)
    return pl.pallas_call(
        functools.partial(_ffn_kernel, final=final),
        grid=(n_tok // t,),
        in_specs=[pl.BlockSpec((t, D_MODEL), rows),
                  _layer_resident((1, D_MODEL), layer),
                  _layer_resident((D_MODEL, 2 * D_FF), layer),
                  _layer_resident((D_FF, D_MODEL), layer),
                  _resident((1, D_MODEL))],
        out_specs=pl.BlockSpec((t, D_MODEL), rows),
        out_shape=jax.ShapeDtypeStruct((n_tok, D_MODEL), _F32),
        compiler_params=_params(("arbitrary",)),
        name=name,
    )(x, n2, w1, w2, fn)


def _rope_tables(positions):
    half = HEAD_DIM // 2
    inv = ROPE_THETA ** (-np.arange(half, dtype=np.float64) / half)
    ang = np.asarray(positions, np.float64)[:, None] * inv[None, :]
    cos = np.cos(ang)
    sin = np.sin(ang)
    cos_t = np.tile(np.concatenate([cos, cos], axis=-1), (1, LANES // HEAD_DIM))
    sin_t = np.tile(np.concatenate([-sin, sin], axis=-1), (1, LANES // HEAD_DIM))
    return jnp.asarray(cos_t, _F32), jnp.asarray(sin_t, _F32)


def kernel(x_prompt, x_sample, cache_k, cache_v, state_conf, state_sconv, norm1, w_in, sinks,
           conf_dw_w, conf_dw_b, conf_ln_g, conf_ln_b, sconv_w, w_branch, w_out, norm2,
           w_ffn_in, w_ffn_out, final_norm):
    depth = w_in.shape[0]
    bp, sp, _ = x_prompt.shape
    bs, ts, _ = x_sample.shape
    cache_len = cache_k.shape[2]
    assert sp % PROMPT_TILE == 0 and PROMPT_TILE % BLOCK == 0 and sp >= WINDOW
    assert bs % SAMPLE_BATCH_TILE == 0 and SAMPLE_BATCH_TILE % SAMPLE_UNROLL == 0
    assert ts % SUBLANES == 0 and cache_len == WINDOW
    assert (bp * sp) % FFN_TILE == 0 and (bs * ts) % FFN_TILE == 0 and D_FF % FFN_CHUNK == 0

    cos_p, sin_p = _rope_tables(np.arange(sp))
    cos_s, sin_s = _rope_tables(np.tile(PAST_LEN + np.arange(ts), SAMPLE_BATCH_TILE))

    xp = x_prompt
    xs = x_sample.reshape(bs * ts, D_MODEL)
    ck = cache_k.transpose(0, 1, 3, 4, 2).reshape(depth, bs, KV_W, cache_len)
    cv = cache_v.transpose(0, 1, 3, 4, 2).reshape(depth, bs, KV_W, cache_len)
    fn = final_norm.reshape(1, D_MODEL)
    prompt_states = [[] for _ in range(4)]
    sample_states = tuple(jnp.zeros(a.shape, _F32) for a in (ck, cv, state_conf, state_sconv))
    lw = (norm1.reshape(depth, 1, D_MODEL), w_in.astype(_BF16), conf_dw_w,
          conf_dw_b.reshape(depth, 1, MIX_W), conf_ln_g.reshape(depth, 1, MIX_W),
          conf_ln_b.reshape(depth, 1, MIX_W), sconv_w, w_branch.astype(_BF16), w_out.astype(_BF16))
    n2 = norm2.reshape(depth, 1, D_MODEL)
    w1 = w_ffn_in.astype(_BF16)
    w2 = w_ffn_out.astype(_BF16)
    for l in range(depth):
        final = l == depth - 1
        xp, kp, vp, cp, scp = _prompt_mixer(l, xp, cos_p, sin_p, sinks[l], lw)
        xp = _ffn(l, xp.reshape(bp * sp, D_MODEL), n2, w1, w2, fn, final, "prompt_ffn").reshape(bp, sp, D_MODEL)
        xs, *sample_states = _sample_mixer(l, xs, cos_s, sin_s, ck, cv, state_conf, state_sconv,
                                           sinks[l], lw, tuple(sample_states))
        xs = _ffn(l, xs, n2, w1, w2, fn, final, "sample_ffn")

        prompt_states[0].append(kp.reshape(bp, WINDOW, N_KV_HEADS, HEAD_DIM))
        prompt_states[1].append(vp.reshape(bp, WINDOW, N_KV_HEADS, HEAD_DIM))
        prompt_states[2].append(cp[:, CONF_HIST - (CONF_K - 1):, :])
        prompt_states[3].append(scp[:, SC_HIST - (SC_K - 1):, :])
    k_s, v_s, c_s, s_s = sample_states
    kv_t_shape = (depth, bs, N_KV_HEADS, HEAD_DIM, cache_len)
    k_s = k_s.reshape(kv_t_shape).transpose(0, 1, 4, 2, 3)
    v_s = v_s.reshape(kv_t_shape).transpose(0, 1, 4, 2, 3)
    return ((xp, xs.reshape(bs, ts, D_MODEL)) + tuple(jnp.stack(o) for o in prompt_states)
            + (k_s, v_s, c_s, s_s))
```

```python
import functools

import numpy as np
import jax
import jax.numpy as jnp
from jax import lax
from jax.experimental import pallas as pl
from jax.experimental.pallas import tpu as pltpu

D_MODEL = 1024
HEAD_DIM = 64
N_HEADS = 8
N_KV_HEADS = 2
GROUP = N_HEADS // N_KV_HEADS
WINDOW = 128
BLOCK = 128
ROPE_THETA = 10000.0
Q_W = N_HEADS * HEAD_DIM
KV_W = N_KV_HEADS * HEAD_DIM
MIX_W = Q_W
CONF_K = 31
SC_K = 3
N_BRANCH = 3
D_FF = 2816
EPS = 1e-6
PAST_LEN = 16384

_O_Q = 0
_O_K = _O_Q + Q_W
_O_V = _O_K + KV_W
_O_CA = _O_V + KV_W
_O_CG = _O_CA + MIX_W
_O_SB = _O_CG + MIX_W
_O_SC = _O_SB + MIX_W
_O_SX = _O_SC + MIX_W
_O_GL = _O_SX + MIX_W
IN_W = _O_GL + N_BRANCH * D_MODEL

SUBLANES = 8
LANES = 128
VMEM_LIMIT_BYTES = 58 * 1024 * 1024

PROMPT_TILE = 256
SAMPLE_BATCH_TILE = 32
SAMPLE_UNROLL = 4
FFN_TILE = 1024
FFN_CHUNK = 256
CONF_HIST = 32
SC_HIST = 8

_BF16 = jnp.bfloat16
_F32 = jnp.float32


def _dot(a, b):
    return jnp.dot(a, b, preferred_element_type=_F32)


def _rmsnorm(x, g):
    return x * lax.rsqrt(jnp.mean(x * x, axis=-1, keepdims=True) + EPS) * g


def _rope(x, cos, sin_signed):
    lane = lax.broadcasted_iota(jnp.int32, (x.shape[0], LANES), 1)
    first_half = (lane % HEAD_DIM) < (HEAD_DIM // 2)
    cols = []
    for c in range(x.shape[1] // LANES):
        xb = x[:, c * LANES:(c + 1) * LANES]
        partner = jnp.where(first_half,
                            pltpu.roll(xb, LANES - HEAD_DIM // 2, 1),
                            pltpu.roll(xb, HEAD_DIM // 2, 1))
        cols.append(xb * cos + partner * sin_signed)
    return cols[0] if len(cols) == 1 else jnp.concatenate(cols, axis=-1)


def _sink_attention(q, k, v, mask, sink_col):
    s = lax.dot_general(q, k, (((1,), (1,)), ((), ())), preferred_element_type=_F32)
    s = jnp.where(mask, s, -jnp.inf)
    m = jnp.maximum(jnp.max(s, axis=-1, keepdims=True), sink_col)
    p = jnp.exp(s - m)
    denom = jnp.sum(p, axis=-1, keepdims=True) + jnp.exp(sink_col - m)
    return _dot(p.astype(_BF16), v) / denom


def _layernorm(x, g, b):
    mu = jnp.mean(x, axis=-1, keepdims=True)
    xc = x - mu
    var = jnp.mean(xc * xc, axis=-1, keepdims=True)
    return xc * lax.rsqrt(var + EPS) * g + b


def _conf_conv_rows(ext_ref, w_ref, r0, rows):
    base = CONF_HIST - (CONF_K - 1)
    out = None
    for r in range(SUBLANES):
        g = None
        for a in range((base + CONF_K - 1) // SUBLANES + 1):
            j = a * SUBLANES + r - base
            if j < 0 or j >= CONF_K:
                continue
            lo = r0 + a * SUBLANES
            term = w_ref[j:j + 1, :] * ext_ref[lo:lo + rows + SUBLANES, :]
            g = term if g is None else g + term
        if g is None:
            continue
        part = g[r:r + rows, :]
        out = part if out is None else out + part
    return out


def _branch_merge(h, win_ref, wb_ref, wout_ref, attn_o, conf_o, sc_o):
    merged = None
    for i, br in enumerate((attn_o, conf_o, sc_o)):
        gate = jax.nn.sigmoid(_dot(h, win_ref[:, _O_GL + i * D_MODEL:_O_GL + (i + 1) * D_MODEL]))
        term = gate * _dot(br.astype(_BF16), wb_ref[i])
        merged = term if merged is None else merged + term
    return _dot(merged.astype(_BF16), wout_ref[...])


def _prompt_layer_kernel(sinks_ref, x_ref, cos_ref, sin_ref, n1_ref, win_ref, dww_ref, dwb_ref,
                         lng_ref, lnb_ref, scw_ref, wb_ref, wout_ref, n2_ref, w1_ref, w2_ref, fn_ref,
                         o_ref, kst_ref, vst_ref, cst_ref, sst_ref,
                         x1_ref, hn_ref, kprev_ref, vprev_ref, cext_ref, sext_ref,
                         *, tiles_per_seq, n_tiles, final):
    t = PROMPT_TILE
    step = pl.program_id(0)
    j = jnp.minimum(step, n_tiles - 1) % tiles_per_seq

    @pl.when(step == 0)
    def _():
        x1_ref[...] = jnp.zeros_like(x1_ref)
        hn_ref[...] = jnp.zeros_like(hn_ref)

    @pl.when(j == 0)
    def _():
        kprev_ref[...] = jnp.zeros_like(kprev_ref)
        vprev_ref[...] = jnp.zeros_like(vprev_ref)
        cext_ref[...] = jnp.zeros_like(cext_ref)
        sext_ref[...] = jnp.zeros_like(sext_ref)

    hn = hn_ref[...]
    ffn = {"acc": x1_ref[...], "c": 0, "gu": None}
    n_chunks = D_FF // FFN_CHUNK

    def gate_up(c):
        lo = c * FFN_CHUNK
        return (_dot(hn, w1_ref[:, lo:lo + FFN_CHUNK]),
                _dot(hn, w1_ref[:, D_FF + lo:D_FF + lo + FFN_CHUNK]))

    def ffn_piece():
        c = ffn["c"]
        if c == n_chunks:
            return
        if ffn["gu"] is None:
            ffn["gu"] = gate_up(0)
        g, up = ffn["gu"]
        act = (jax.nn.silu(g) * up).astype(_BF16)
        if c + 1 < n_chunks:
            ffn["gu"] = gate_up(c + 1)
        ffn["acc"] = ffn["acc"] + _dot(act, w2_ref[c * FFN_CHUNK:(c + 1) * FFN_CHUNK, :])
        ffn["c"] = c + 1

    x = x_ref[...]
    h = _rmsnorm(x, n1_ref[...]).astype(_BF16)

    ffn_piece()
    qkv = _dot(h, win_ref[:, _O_Q:_O_CA])
    ffn_piece()
    cacg = _dot(h, win_ref[:, _O_CA:_O_SB])

    cos = cos_ref[...]
    sin = sin_ref[...]
    q = _rope(qkv[:, _O_Q:_O_K], cos, sin) * (HEAD_DIM ** -0.5)
    k = _rope(qkv[:, _O_K:_O_V], cos, sin)
    v = qkv[:, _O_V:_O_CA]
    kst_ref[0] = k[t - WINDOW:, :]
    vst_ref[0] = v[t - WINDOW:, :]
    qb = q.astype(_BF16)
    kb = jnp.concatenate([kprev_ref[...], k.astype(_BF16)], axis=0)
    vb = jnp.concatenate([vprev_ref[...], v.astype(_BF16)], axis=0)
    kprev_ref[...] = kb[t:, :]
    vprev_ref[...] = vb[t:, :]

    qi = lax.broadcasted_iota(jnp.int32, (BLOCK, 2 * BLOCK), 0)
    kj = lax.broadcasted_iota(jnp.int32, (BLOCK, 2 * BLOCK), 1)
    diff = qi + BLOCK - kj
    band = (diff >= 0) & (diff < WINDOW)
    band4 = jnp.concatenate([band] * GROUP, axis=0)
    first_key = jnp.where(j == 0, BLOCK, 0)
    first_mask = band4 & jnp.concatenate([kj >= first_key] * GROUP, axis=0)

    sink_cols = [jnp.concatenate(
        [jnp.full((BLOCK, 1), sinks_ref[kh * GROUP + g], _F32) for g in range(GROUP)], axis=0)
        for kh in range(N_KV_HEADS)]

    def scores(blk):
        r0 = blk * BLOCK
        out = []
        for kh in range(N_KV_HEADS):
            qs = jnp.concatenate(
                [qb[r0:r0 + BLOCK, (kh * GROUP + g) * HEAD_DIM:(kh * GROUP + g + 1) * HEAD_DIM]
                 for g in range(GROUP)], axis=0)
            out.append(lax.dot_general(qs, kb[r0:r0 + 2 * BLOCK, kh * HEAD_DIM:(kh + 1) * HEAD_DIM],
                                       (((1,), (1,)), ((), ())), preferred_element_type=_F32))
        return out

    def softmax(blk, s_list):
        mask = first_mask if blk == 0 else band4
        out = []
        for kh in range(N_KV_HEADS):
            s = jnp.where(mask, s_list[kh], -jnp.inf)
            m = jnp.maximum(jnp.max(s, axis=-1, keepdims=True), sink_cols[kh])
            p = jnp.exp(s - m)
            denom = jnp.sum(p, axis=-1, keepdims=True) + jnp.exp(sink_cols[kh] - m)
            out.append((p.astype(_BF16), denom))
        return out

    def weighted_values(blk, p_list):
        r0 = blk * BLOCK
        heads = [None] * N_HEADS
        for kh in range(N_KV_HEADS):
            p, denom = p_list[kh]
            o = _dot(p, vb[r0:r0 + 2 * BLOCK, kh * HEAD_DIM:(kh + 1) * HEAD_DIM]) / denom
            for g in range(GROUP):
                heads[kh * GROUP + g] = o[g * BLOCK:(g + 1) * BLOCK, :]
        return jnp.concatenate(heads, axis=-1)

    n_blk = t // BLOCK

    u = cacg[:, :MIX_W] * jax.nn.sigmoid(cacg[:, MIX_W:])
    cext_ref[CONF_HIST:CONF_HIST + t, :] = u
    cst_ref[0] = u[t - CONF_HIST:, :]
    dw = _conf_conv_rows(cext_ref, dww_ref, 0, t)
    cext_ref[0:CONF_HIST, :] = u[t - CONF_HIST:, :]
    conf_o = jax.nn.silu(_layernorm(dw + dwb_ref[...], lng_ref[...], lnb_ref[...]))

    wide = [(_O_SC, _O_GL), (_O_GL, _O_GL + D_MODEL), (_O_GL + D_MODEL, _O_GL + 2 * D_MODEL),
            (_O_GL + 2 * D_MODEL, _O_GL + 3 * D_MODEL), (_O_SB, _O_SC)]
    wide_out = []
    attn_rows = []
    s_cur = scores(0)
    for blk in range(n_blk):
        if len(wide_out) < len(wide):
            lo, hi = wide[len(wide_out)]
            wide_out.append(_dot(h, win_ref[:, lo:hi]))
        ffn_piece()
        p_cur = softmax(blk, s_cur)
        if blk + 1 < n_blk:
            s_cur = scores(blk + 1)
        attn_rows.append(weighted_values(blk, p_cur))
    while len(wide_out) < len(wide):
        lo, hi = wide[len(wide_out)]
        wide_out.append(_dot(h, win_ref[:, lo:hi]))
        ffn_piece()
    scsx, g0, g1, g2, sb = wide_out
    attn_o = attn_rows[0] if len(attn_rows) == 1 else jnp.concatenate(attn_rows, axis=0)

    cu = scsx[:, :MIX_W] * scsx[:, MIX_W:]
    sext_ref[SC_HIST:SC_HIST + t, :] = cu
    ext = sext_ref[...]
    conv3 = None
    for jj in range(SC_K):
        off = SC_HIST - (SC_K - 1) + jj
        term = scw_ref[jj:jj + 1, :] * ext[off:off + t, :]
        conv3 = term if conv3 is None else conv3 + term
    sst_ref[0] = cu[t - SC_HIST:, :]
    sext_ref[0:SC_HIST, :] = cu[t - SC_HIST:, :]
    sc_o = sb * conv3

    m_sc = jax.nn.sigmoid(g2) * _dot(sc_o.astype(_BF16), wb_ref[2])
    ffn_piece()
    m_attn = jax.nn.sigmoid(g0) * _dot(attn_o.astype(_BF16), wb_ref[0])
    ffn_piece()
    m_conf = jax.nn.sigmoid(g1) * _dot(conf_o.astype(_BF16), wb_ref[1])
    merged = m_attn + m_conf + m_sc
    x1_new = x + _dot(merged.astype(_BF16), wout_ref[...])
    while ffn["c"] < n_chunks:
        ffn_piece()
    acc = ffn["acc"]
    if final:
        acc = _rmsnorm(acc, fn_ref[...])
    o_ref[...] = acc
    x1_ref[...] = x1_new
    hn_ref[...] = _rmsnorm(x1_new, n2_ref[...]).astype(_BF16)


def _sample_mixer_kernel(*refs, n_carried):
    (sinks_ref, x_ref, cos_ref, sin_ref, ck_ref, cv_ref, sconf_ref, ssc_ref,
     n1_ref, win_ref, dww_ref, dwb_ref, lng_ref, lnb_ref, scw_ref, wb_ref, wout_ref) = refs[:17]
    (xo_ref, ko_ref, vo_ref, co_ref, so_ref,
     q_s, k_s, v_s, u_s, cu_s, attn_s, dw_s, c3_s, cext_s, sext_s) = refs[17 + n_carried:]
    bc = SAMPLE_BATCH_TILE
    ts = x_ref.shape[0] // bc
    x = x_ref[...]
    h = _rmsnorm(x, n1_ref[...]).astype(_BF16)
    cos = cos_ref[...]
    sin = sin_ref[...]
    qkv = _dot(h, win_ref[:, _O_Q:_O_CA])
    cacg = _dot(h, win_ref[:, _O_CA:_O_SB])
    scsx = _dot(h, win_ref[:, _O_SC:_O_GL])
    q_s[...] = _rope(qkv[:, _O_Q:_O_K], cos, sin) * (HEAD_DIM ** -0.5)
    k_s[...] = _rope(qkv[:, _O_K:_O_V], cos, sin)
    v_s[...] = qkv[:, _O_V:_O_CA]
    u_s[...] = cacg[:, :MIX_W] * jax.nn.sigmoid(cacg[:, MIX_W:])
    cu_s[...] = scsx[:, :MIX_W] * scsx[:, MIX_W:]

    cache_len = ck_ref.shape[2]
    tok = lax.broadcasted_iota(jnp.int32, (GROUP * ts, 2 * cache_len), 0) % ts
    lane2 = lax.broadcasted_iota(jnp.int32, (GROUP * ts, 2 * cache_len), 1)
    new_tok = lane2 - (2 * cache_len - ts)
    mask = ((lane2 < ts) & (lane2 > tok)) | ((lane2 >= cache_len) & (new_tok <= tok))
    new_lanes = lax.broadcasted_iota(jnp.int32, (KV_W, cache_len), 1) >= cache_len - ts
    pad_rows = jnp.zeros((cache_len - ts, KV_W), _F32)
    sink_cols = [jnp.concatenate(
        [jnp.full((ts, 1), sinks_ref[kh * GROUP + g], _F32) for g in range(GROUP)], axis=0)
        for kh in range(N_KV_HEADS)]

    def attention(bs_):
        rows = [pl.ds(pl.multiple_of(b * ts, ts), ts) for b in bs_]
        vbs, scores = [], []
        def shifted_in(old_t, new_rows):
            new_t = jnp.concatenate([pad_rows, new_rows], axis=0).T
            return jnp.where(new_lanes, new_t, pltpu.roll(old_t, cache_len - ts, 1))

        for b, rw in zip(bs_, rows):
            k_old = ck_ref[b]
            v_old = cv_ref[b]
            k_out = shifted_in(k_old, k_s[rw, :])
            v_out = shifted_in(v_old, v_s[rw, :])
            ko_ref[b] = k_out
            vo_ref[b] = v_out
            kb = jnp.concatenate([k_old, k_out], axis=1).astype(_BF16)
            vbs.append(jnp.concatenate([v_old, v_out], axis=1).astype(_BF16))
            qrow = q_s[rw, :].astype(_BF16)
            for kh in range(N_KV_HEADS):
                qs = jnp.concatenate(
                    [qrow[:, (kh * GROUP + g) * HEAD_DIM:(kh * GROUP + g + 1) * HEAD_DIM]
                     for g in range(GROUP)], axis=0)
                scores.append(_dot(qs, kb[kh * HEAD_DIM:(kh + 1) * HEAD_DIM, :]))
        probs = []
        for i, s in enumerate(scores):
            sink_col = sink_cols[i % N_KV_HEADS]
            s = jnp.where(mask, s, -jnp.inf)
            m = jnp.maximum(jnp.max(s, axis=-1, keepdims=True), sink_col)
            p = jnp.exp(s - m)
            probs.append((p.astype(_BF16), jnp.sum(p, axis=-1, keepdims=True) + jnp.exp(sink_col - m)))
        for n, rw in enumerate(rows):
            heads = [None] * N_HEADS
            for kh in range(N_KV_HEADS):
                p, denom = probs[n * N_KV_HEADS + kh]
                o = lax.dot_general(p, vbs[n][kh * HEAD_DIM:(kh + 1) * HEAD_DIM, :],
                                    (((1,), (1,)), ((), ())), preferred_element_type=_F32) / denom
                for g in range(GROUP):
                    heads[kh * GROUP + g] = o[g * ts:(g + 1) * ts, :]
            attn_s[rw, :] = jnp.concatenate(heads, axis=-1)

    def convs(b, slot):
        rows = pl.ds(pl.multiple_of(b * ts, ts), ts)
        cext = cext_s.at[slot]
        cext[0:CONF_K - 1, :] = sconf_ref[b]
        cext[CONF_K - 1:CONF_K - 1 + ts, :] = u_s[rows, :]
        dw = None
        for jj in range(CONF_K):
            term = dww_ref[jj:jj + 1, :] * cext[jj:jj + ts, :]
            dw = term if dw is None else dw + term
        dw_s[rows, :] = dw
        co_ref[b] = cext[ts:ts + CONF_K - 1, :]

        sext = sext_s.at[slot]
        sext[0:SC_K - 1, :] = ssc_ref[b]
        sext[SC_K - 1:SC_K - 1 + ts, :] = cu_s[rows, :]
        c3 = None
        for jj in range(SC_K):
            term = scw_ref[jj:jj + 1, :] * sext[jj:jj + ts, :]
            c3 = term if c3 is None else c3 + term
        c3_s[rows, :] = c3
        so_ref[b] = sext[ts:ts + SC_K - 1, :]

    def body(i, carry):
        group = [i * SAMPLE_UNROLL + slot for slot in range(SAMPLE_UNROLL)]
        attention(group)
        for slot, b in enumerate(group):
            convs(b, slot)
        return carry

    lax.fori_loop(0, bc // SAMPLE_UNROLL, body, 0)

    conf_o = jax.nn.silu(_layernorm(dw_s[...] + dwb_ref[...], lng_ref[...], lnb_ref[...]))
    sc_o = _dot(h, win_ref[:, _O_SB:_O_SC]) * c3_s[...]
    xo_ref[...] = x + _branch_merge(h, win_ref, wb_ref, wout_ref, attn_s[...], conf_o, sc_o)


def _ffn_kernel(x_ref, n2_ref, w1_ref, w2_ref, fn_ref, o_ref, *, final):
    x = x_ref[...]
    hn = _rmsnorm(x, n2_ref[...]).astype(_BF16)

    def gate_up(c):
        lo = c * FFN_CHUNK
        return (_dot(hn, w1_ref[:, lo:lo + FFN_CHUNK]),
                _dot(hn, w1_ref[:, D_FF + lo:D_FF + lo + FFN_CHUNK]))

    acc = x
    n_chunks = D_FF // FFN_CHUNK
    g, up = gate_up(0)
    for c in range(n_chunks):
        act = (jax.nn.silu(g) * up).astype(_BF16)
        if c + 1 < n_chunks:
            g, up = gate_up(c + 1)
        acc = acc + _dot(act, w2_ref[c * FFN_CHUNK:(c + 1) * FFN_CHUNK, :])
    if final:
        acc = _rmsnorm(acc, fn_ref[...])
    o_ref[...] = acc


def _resident(shape):
    nd = len(shape)
    return pl.BlockSpec(shape, lambda *_: (0,) * nd, pipeline_mode=pl.Buffered(1))


def _smem():
    return pl.BlockSpec(memory_space=pltpu.SMEM)


def _params(semantics):
    return pltpu.CompilerParams(dimension_semantics=semantics, vmem_limit_bytes=VMEM_LIMIT_BYTES)


def _layer_resident(shape, layer):
    nd = len(shape)
    return pl.BlockSpec((None,) + shape, lambda *_: (layer,) + (0,) * nd, pipeline_mode=pl.Buffered(1))


def _layer_weight_specs(layer):
    return [
        _layer_resident((1, D_MODEL), layer),
        _layer_resident((D_MODEL, IN_W), layer),
        _layer_resident((CONF_K, MIX_W), layer),
        _layer_resident((1, MIX_W), layer),
        _layer_resident((1, MIX_W), layer),
        _layer_resident((1, MIX_W), layer),
        _layer_resident((SC_K, MIX_W), layer),
        _layer_resident((N_BRANCH, MIX_W, D_MODEL), layer),
        _layer_resident((D_MODEL, D_MODEL), layer),
    ]


def _prompt_layer(layer, x, cos, sin, sinks, lw, n2, w1, w2, fn, final):
    n_tok = x.shape[0]
    s = cos.shape[0]
    b = n_tok // s
    t = PROMPT_TILE
    tps = s // t
    n_tiles = n_tok // t
    cur = lambda i: jnp.minimum(i, n_tiles - 1)
    prev = lambda i: jnp.maximum(i - 1, 0)
    state = lambda i: (cur(i) // tps, 0, 0)
    return pl.pallas_call(
        functools.partial(_prompt_layer_kernel, tiles_per_seq=tps, n_tiles=n_tiles, final=final),
        grid=(n_tiles + 1,),
        in_specs=[_smem(),
                  pl.BlockSpec((t, D_MODEL), lambda i: (cur(i), 0)),
                  pl.BlockSpec((t, LANES), lambda i: (cur(i) % tps, 0)),
                  pl.BlockSpec((t, LANES), lambda i: (cur(i) % tps, 0))] + _layer_weight_specs(layer)
                 + [_layer_resident((1, D_MODEL), layer),
                    _layer_resident((D_MODEL, 2 * D_FF), layer),
                    _layer_resident((D_FF, D_MODEL), layer),
                    _resident((1, D_MODEL))],
        out_specs=[pl.BlockSpec((t, D_MODEL), lambda i: (prev(i), 0)),
                   pl.BlockSpec((1, WINDOW, KV_W), state),
                   pl.BlockSpec((1, WINDOW, KV_W), state),
                   pl.BlockSpec((1, CONF_HIST, MIX_W), state),
                   pl.BlockSpec((1, SC_HIST, MIX_W), state)],
        out_shape=[jax.ShapeDtypeStruct((n_tok, D_MODEL), _F32),
                   jax.ShapeDtypeStruct((b, WINDOW, KV_W), _F32),
                   jax.ShapeDtypeStruct((b, WINDOW, KV_W), _F32),
                   jax.ShapeDtypeStruct((b, CONF_HIST, MIX_W), _F32),
                   jax.ShapeDtypeStruct((b, SC_HIST, MIX_W), _F32)],
        scratch_shapes=[pltpu.VMEM((t, D_MODEL), _F32),
                        pltpu.VMEM((t, D_MODEL), _BF16),
                        pltpu.VMEM((BLOCK, KV_W), _BF16),
                        pltpu.VMEM((BLOCK, KV_W), _BF16),
                        pltpu.VMEM((CONF_HIST + t + SUBLANES, MIX_W), _F32),
                        pltpu.VMEM((SC_HIST + t, MIX_W), _F32)],
        compiler_params=_params(("arbitrary",)),
        name="prompt_layer",
    )(sinks, x, cos, sin, *lw, n2, w1, w2, fn)


def _sample_mixer(layer, x, cos, sin, ck, cv, sconf, ssc, sinks, lw, carried):
    n_tok = x.shape[0]
    bc = SAMPLE_BATCH_TILE
    nb, cache_len = ck.shape[1], ck.shape[3]
    ts = n_tok // nb
    r = bc * ts
    rows = lambda i: (i, 0)
    seqs = lambda i: (layer, i, 0, 0)
    state_specs = [pl.BlockSpec((None, bc, KV_W, cache_len), seqs),
                   pl.BlockSpec((None, bc, KV_W, cache_len), seqs),
                   pl.BlockSpec((None, bc, CONF_K - 1, MIX_W), seqs),
                   pl.BlockSpec((None, bc, SC_K - 1, MIX_W), seqs)]
    n_fixed_in = 8 + len(lw)
    return pl.pallas_call(
        functools.partial(_sample_mixer_kernel, n_carried=len(carried)),
        grid=(nb // bc,),
        in_specs=[_smem(),
                  pl.BlockSpec((r, D_MODEL), rows),
                  _resident((r, LANES)),
                  _resident((r, LANES))] + state_specs + _layer_weight_specs(layer)
                 + [pl.BlockSpec(memory_space=pl.ANY)] * len(carried),
        out_specs=[pl.BlockSpec((r, D_MODEL), rows)] + state_specs,
        out_shape=[jax.ShapeDtypeStruct((n_tok, D_MODEL), _F32),
                   jax.ShapeDtypeStruct(ck.shape, _F32),
                   jax.ShapeDtypeStruct(cv.shape, _F32),
                   jax.ShapeDtypeStruct(sconf.shape, _F32),
                   jax.ShapeDtypeStruct(ssc.shape, _F32)],
        input_output_aliases={n_fixed_in + i: 1 + i for i in range(len(carried))},
        scratch_shapes=[pltpu.VMEM((r, Q_W), _F32),
                        pltpu.VMEM((r, KV_W), _F32),
                        pltpu.VMEM((r, KV_W), _F32),
                        pltpu.VMEM((r, MIX_W), _F32),
                        pltpu.VMEM((r, MIX_W), _F32),
                        pltpu.VMEM((r, Q_W), _F32),
                        pltpu.VMEM((r, MIX_W), _F32),
                        pltpu.VMEM((r, MIX_W), _F32),
                        pltpu.VMEM((SAMPLE_UNROLL, CONF_K - 1 + ts, MIX_W), _F32),
                        pltpu.VMEM((SAMPLE_UNROLL, SC_K - 1 + ts, MIX_W), _F32)],
        compiler_params=_params(("arbitrary",)),
        name="sample_mixer",
    )(sinks, x, cos, sin, ck, cv, sconf, ssc, *lw, *carried)


def _ffn(layer, x, n2, w1, w2, fn, final, name):
    n_tok = x.shape[0]
    t = FFN_TILE
    rows = lambda i: (i, 0)
    return pl.pallas_call(
        functools.partial(_ffn_kernel, final=final),
        grid=(n_tok // t,),
        in_specs=[pl.BlockSpec((t, D_MODEL), rows),
                  _layer_resident((1, D_MODEL), layer),
                  _layer_resident((D_MODEL, 2 * D_FF), layer),
                  _layer_resident((D_FF, D_MODEL), layer),
                  _resident((1, D_MODEL))],
        out_specs=pl.BlockSpec((t, D_MODEL), rows),
        out_shape=jax.ShapeDtypeStruct((n_tok, D_MODEL), _F32),
        compiler_params=_params(("arbitrary",)),
        name=name,
    )(x, n2, w1, w2, fn)


def _rope_tables(positions):
    half = HEAD_DIM // 2
    inv = ROPE_THETA ** (-np.arange(half, dtype=np.float64) / half)
    ang = np.asarray(positions, np.float64)[:, None] * inv[None, :]
    cos = np.cos(ang)
    sin = np.sin(ang)
    cos_t = np.tile(np.concatenate([cos, cos], axis=-1), (1, LANES // HEAD_DIM))
    sin_t = np.tile(np.concatenate([-sin, sin], axis=-1), (1, LANES // HEAD_DIM))
    return jnp.asarray(cos_t, _F32), jnp.asarray(sin_t, _F32)


def kernel(x_prompt, x_sample, cache_k, cache_v, state_conf, state_sconv, norm1, w_in, sinks,
           conf_dw_w, conf_dw_b, conf_ln_g, conf_ln_b, sconv_w, w_branch, w_out, norm2,
           w_ffn_in, w_ffn_out, final_norm):
    depth = w_in.shape[0]
    bp, sp, _ = x_prompt.shape
    bs, ts, _ = x_sample.shape
    cache_len = cache_k.shape[2]
    assert sp % PROMPT_TILE == 0 and PROMPT_TILE % BLOCK == 0 and sp >= WINDOW
    assert bs % SAMPLE_BATCH_TILE == 0 and SAMPLE_BATCH_TILE % SAMPLE_UNROLL == 0
    assert ts % SUBLANES == 0 and cache_len == WINDOW
    assert (bs * ts) % FFN_TILE == 0 and D_FF % FFN_CHUNK == 0

    cos_p, sin_p = _rope_tables(np.arange(sp))
    cos_s, sin_s = _rope_tables(np.tile(PAST_LEN + np.arange(ts), SAMPLE_BATCH_TILE))

    xp = x_prompt.reshape(bp * sp, D_MODEL)
    xs = x_sample.reshape(bs * ts, D_MODEL)
    ck = cache_k.transpose(0, 1, 3, 4, 2).reshape(depth, bs, KV_W, cache_len)
    cv = cache_v.transpose(0, 1, 3, 4, 2).reshape(depth, bs, KV_W, cache_len)
    fn = final_norm.reshape(1, D_MODEL)
    prompt_states = [[] for _ in range(4)]
    sample_states = tuple(jnp.zeros(a.shape, _F32) for a in (ck, cv, state_conf, state_sconv))
    lw = (norm1.reshape(depth, 1, D_MODEL), w_in.astype(_BF16), conf_dw_w,
          conf_dw_b.reshape(depth, 1, MIX_W), conf_ln_g.reshape(depth, 1, MIX_W),
          conf_ln_b.reshape(depth, 1, MIX_W), sconv_w, w_branch.astype(_BF16), w_out.astype(_BF16))
    n2 = norm2.reshape(depth, 1, D_MODEL)
    w1 = w_ffn_in.astype(_BF16)
    w2 = w_ffn_out.astype(_BF16)
    for l in range(depth):
        final = l == depth - 1
        xp, kp, vp, cp, scp = _prompt_layer(l, xp, cos_p, sin_p, sinks[l], lw, n2, w1, w2, fn, final)
        xs, *sample_states = _sample_mixer(l, xs, cos_s, sin_s, ck, cv, state_conf, state_sconv,
                                           sinks[l], lw, tuple(sample_states))
        xs = _ffn(l, xs, n2, w1, w2, fn, final, "sample_ffn")

        prompt_states[0].append(kp.reshape(bp, WINDOW, N_KV_HEADS, HEAD_DIM))
        prompt_states[1].append(vp.reshape(bp, WINDOW, N_KV_HEADS, HEAD_DIM))
        prompt_states[2].append(cp[:, CONF_HIST - (CONF_K - 1):, :])
        prompt_states[3].append(scp[:, SC_HIST - (SC_K - 1):, :])
    k_s, v_s, c_s, s_s = sample_states
    kv_t_shape = (depth, bs, N_KV_HEADS, HEAD_DIM, cache_len)
    k_s = k_s.reshape(kv_t_shape).transpose(0, 1, 4, 2, 3)
    v_s = v_s.reshape(kv_t_shape).transpose(0, 1, 4, 2, 3)
    return ((xp.reshape(bp, sp, D_MODEL), xs.reshape(bs, ts, D_MODEL)) + tuple(jnp.stack(o) for o in prompt_states)
            + (k_s, v_s, c_s, s_s))
```

```python
import functools

import numpy as np
import jax
import jax.numpy as jnp
from jax import lax
from jax.experimental import pallas as pl
from jax.experimental.pallas import tpu as pltpu

D_MODEL = 1024
HEAD_DIM = 64
N_HEADS = 8
N_KV_HEADS = 2
GROUP = N_HEADS // N_KV_HEADS
WINDOW = 128
BLOCK = 128
ROPE_THETA = 10000.0
Q_W = N_HEADS * HEAD_DIM
KV_W = N_KV_HEADS * HEAD_DIM
MIX_W = Q_W
CONF_K = 31
SC_K = 3
N_BRANCH = 3
D_FF = 2816
EPS = 1e-6
LOG2E = 1.4426950408889634
PAST_LEN = 16384

_O_Q = 0
_O_K = _O_Q + Q_W
_O_V = _O_K + KV_W
_O_CA = _O_V + KV_W
_O_CG = _O_CA + MIX_W
_O_SB = _O_CG + MIX_W
_O_SC = _O_SB + MIX_W
_O_SX = _O_SC + MIX_W
_O_GL = _O_SX + MIX_W
IN_W = _O_GL + N_BRANCH * D_MODEL

SUBLANES = 8
LANES = 128
VMEM_LIMIT_BYTES = 58 * 1024 * 1024

PROMPT_TILE = 256
SAMPLE_BATCH_TILE = 32
SAMPLE_UNROLL = 4
FFN_TILE = 1024
FFN_CHUNK = 256
CONF_HIST = 32
SC_HIST = 8

_BF16 = jnp.bfloat16
_F32 = jnp.float32


def _dot(a, b):
    return jnp.dot(a, b, preferred_element_type=_F32)


def _rmsnorm(x, g):
    return x * lax.rsqrt(jnp.mean(x * x, axis=-1, keepdims=True) + EPS) * g


def _rope(x, cos, sin_signed):
    lane = lax.broadcasted_iota(jnp.int32, (x.shape[0], LANES), 1)
    first_half = (lane % HEAD_DIM) < (HEAD_DIM // 2)
    cols = []
    for c in range(x.shape[1] // LANES):
        xb = x[:, c * LANES:(c + 1) * LANES]
        partner = jnp.where(first_half,
                            pltpu.roll(xb, LANES - HEAD_DIM // 2, 1),
                            pltpu.roll(xb, HEAD_DIM // 2, 1))
        cols.append(xb * cos + partner * sin_signed)
    return cols[0] if len(cols) == 1 else jnp.concatenate(cols, axis=-1)


def _sink_attention(q, k, v, mask, sink_col):
    s = lax.dot_general(q, k, (((1,), (1,)), ((), ())), preferred_element_type=_F32)
    s = jnp.where(mask, s, -jnp.inf)
    m = jnp.maximum(jnp.max(s, axis=-1, keepdims=True), sink_col)
    p = jnp.exp(s - m)
    denom = jnp.sum(p, axis=-1, keepdims=True) + jnp.exp(sink_col - m)
    return _dot(p.astype(_BF16), v) / denom


def _layernorm(x, g, b):
    mu = jnp.mean(x, axis=-1, keepdims=True)
    xc = x - mu
    var = jnp.mean(xc * xc, axis=-1, keepdims=True)
    return xc * lax.rsqrt(var + EPS) * g + b


def _conf_conv_rows(ext_ref, w_ref, r0, rows):
    base = CONF_HIST - (CONF_K - 1)
    out = None
    for r in range(SUBLANES):
        g = None
        for a in range((base + CONF_K - 1) // SUBLANES + 1):
            j = a * SUBLANES + r - base
            if j < 0 or j >= CONF_K:
                continue
            lo = r0 + a * SUBLANES
            term = w_ref[j:j + 1, :] * ext_ref[lo:lo + rows + SUBLANES, :]
            g = term if g is None else g + term
        if g is None:
            continue
        part = g[r:r + rows, :]
        out = part if out is None else out + part
    return out


def _branch_merge(h, win_ref, wb_ref, wout_ref, attn_o, conf_o, sc_o):
    merged = None
    for i, br in enumerate((attn_o, conf_o, sc_o)):
        gate = jax.nn.sigmoid(_dot(h, win_ref[:, _O_GL + i * D_MODEL:_O_GL + (i + 1) * D_MODEL]))
        term = gate * _dot(br.astype(_BF16), wb_ref[i])
        merged = term if merged is None else merged + term
    return _dot(merged.astype(_BF16), wout_ref[...])


def _prompt_layer_kernel(sinks_ref, x_ref, cos_ref, sin_ref, n1_ref, win_ref, dww_ref, dwb_ref,
                         lng_ref, lnb_ref, scw_ref, wb_ref, wout_ref, n2_ref, w1_ref, w2_ref, fn_ref,
                         o_ref, kst_ref, vst_ref, cst_ref, sst_ref,
                         x1_ref, hn_ref, kprev_ref, vprev_ref, cext_ref, sext_ref,
                         *, tiles_per_seq, n_tiles, final):
    t = PROMPT_TILE
    step = pl.program_id(0)
    j = jnp.minimum(step, n_tiles - 1) % tiles_per_seq

    @pl.when(step == 0)
    def _():
        x1_ref[...] = jnp.zeros_like(x1_ref)
        hn_ref[...] = jnp.zeros_like(hn_ref)

    @pl.when(j == 0)
    def _():
        kprev_ref[...] = jnp.zeros_like(kprev_ref)
        vprev_ref[...] = jnp.zeros_like(vprev_ref)
        cext_ref[...] = jnp.zeros_like(cext_ref)
        sext_ref[...] = jnp.zeros_like(sext_ref)

    hn = hn_ref[...]
    ffn = {"acc": x1_ref[...], "c": 0, "gu": None}
    n_chunks = D_FF // FFN_CHUNK

    def gate_up(c):
        lo = c * FFN_CHUNK
        return (_dot(hn, w1_ref[:, lo:lo + FFN_CHUNK]),
                _dot(hn, w1_ref[:, D_FF + lo:D_FF + lo + FFN_CHUNK]))

    def ffn_piece():
        c = ffn["c"]
        if c == n_chunks:
            return
        if ffn["gu"] is None:
            ffn["gu"] = gate_up(0)
        g, up = ffn["gu"]
        act = (jax.nn.silu(g) * up).astype(_BF16)
        if c + 1 < n_chunks:
            ffn["gu"] = gate_up(c + 1)
        ffn["acc"] = ffn["acc"] + _dot(act, w2_ref[c * FFN_CHUNK:(c + 1) * FFN_CHUNK, :])
        ffn["c"] = c + 1

    x = x_ref[...]
    h = _rmsnorm(x, n1_ref[...]).astype(_BF16)

    ffn_piece()
    qkv = _dot(h, win_ref[:, _O_Q:_O_CA])
    ffn_piece()
    cacg = _dot(h, win_ref[:, _O_CA:_O_SB])

    cos = cos_ref[...]
    sin = sin_ref[...]
    q = _rope(qkv[:, _O_Q:_O_K], cos, sin) * (HEAD_DIM ** -0.5 * LOG2E)
    k = _rope(qkv[:, _O_K:_O_V], cos, sin)
    v = qkv[:, _O_V:_O_CA]
    kst_ref[0] = k[t - WINDOW:, :]
    vst_ref[0] = v[t - WINDOW:, :]
    qb = q.astype(_BF16)
    kb = jnp.concatenate([kprev_ref[...], k.astype(_BF16)], axis=0)
    vb = jnp.concatenate([vprev_ref[...], v.astype(_BF16)], axis=0)
    kprev_ref[...] = kb[t:, :]
    vprev_ref[...] = vb[t:, :]

    qi = lax.broadcasted_iota(jnp.int32, (BLOCK, 2 * BLOCK), 0)
    kj = lax.broadcasted_iota(jnp.int32, (BLOCK, 2 * BLOCK), 1)
    diff = qi + BLOCK - kj
    band = (diff >= 0) & (diff < WINDOW)
    band4 = jnp.concatenate([band] * GROUP, axis=0)
    first_key = jnp.where(j == 0, BLOCK, 0)
    first_mask = band4 & jnp.concatenate([kj >= first_key] * GROUP, axis=0)

    sink_cols = [jnp.concatenate(
        [jnp.full((BLOCK, 1), sinks_ref[kh * GROUP + g] * LOG2E, _F32) for g in range(GROUP)], axis=0)
        for kh in range(N_KV_HEADS)]

    def scores(blk):
        r0 = blk * BLOCK
        out = []
        for kh in range(N_KV_HEADS):
            qs = jnp.concatenate(
                [qb[r0:r0 + BLOCK, (kh * GROUP + g) * HEAD_DIM:(kh * GROUP + g + 1) * HEAD_DIM]
                 for g in range(GROUP)], axis=0)
            out.append(lax.dot_general(qs, kb[r0:r0 + 2 * BLOCK, kh * HEAD_DIM:(kh + 1) * HEAD_DIM],
                                       (((1,), (1,)), ((), ())), preferred_element_type=_F32))
        return out

    def softmax(blk, s_list):
        mask = first_mask if blk == 0 else band4
        out = []
        for kh in range(N_KV_HEADS):
            s = jnp.where(mask, s_list[kh], -jnp.inf)
            m = jnp.maximum(jnp.max(s, axis=-1, keepdims=True), sink_cols[kh])
            p = jnp.exp2(s - m)
            denom = jnp.sum(p, axis=-1, keepdims=True) + jnp.exp2(sink_cols[kh] - m)
            out.append((p.astype(_BF16), denom))
        return out

    def weighted_values(blk, p_list):
        r0 = blk * BLOCK
        heads = [None] * N_HEADS
        for kh in range(N_KV_HEADS):
            p, denom = p_list[kh]
            o = _dot(p, vb[r0:r0 + 2 * BLOCK, kh * HEAD_DIM:(kh + 1) * HEAD_DIM]) / denom
            for g in range(GROUP):
                heads[kh * GROUP + g] = o[g * BLOCK:(g + 1) * BLOCK, :]
        return jnp.concatenate(heads, axis=-1)

    n_blk = t // BLOCK

    u = cacg[:, :MIX_W] * jax.nn.sigmoid(cacg[:, MIX_W:])
    cext_ref[CONF_HIST:CONF_HIST + t, :] = u
    cst_ref[0] = u[t - CONF_HIST:, :]
    dw = _conf_conv_rows(cext_ref, dww_ref, 0, t)
    cext_ref[0:CONF_HIST, :] = u[t - CONF_HIST:, :]
    conf_o = jax.nn.silu(_layernorm(dw + dwb_ref[...], lng_ref[...], lnb_ref[...]))

    wide = [(_O_SC, _O_GL), (_O_GL, _O_GL + D_MODEL), (_O_GL + D_MODEL, _O_GL + 2 * D_MODEL),
            (_O_GL + 2 * D_MODEL, _O_GL + 3 * D_MODEL), (_O_SB, _O_SC)]
    wide_out = []
    attn_rows = []
    s_cur = scores(0)
    for blk in range(n_blk):
        if len(wide_out) < len(wide):
            lo, hi = wide[len(wide_out)]
            wide_out.append(_dot(h, win_ref[:, lo:hi]))
        ffn_piece()
        p_cur = softmax(blk, s_cur)
        if blk + 1 < n_blk:
            s_cur = scores(blk + 1)
        attn_rows.append(weighted_values(blk, p_cur))
    while len(wide_out) < len(wide):
        lo, hi = wide[len(wide_out)]
        wide_out.append(_dot(h, win_ref[:, lo:hi]))
        ffn_piece()
    scsx, g0, g1, g2, sb = wide_out
    attn_o = attn_rows[0] if len(attn_rows) == 1 else jnp.concatenate(attn_rows, axis=0)

    cu = scsx[:, :MIX_W] * scsx[:, MIX_W:]
    sext_ref[SC_HIST:SC_HIST + t, :] = cu
    ext = sext_ref[...]
    conv3 = None
    for jj in range(SC_K):
        off = SC_HIST - (SC_K - 1) + jj
        term = scw_ref[jj:jj + 1, :] * ext[off:off + t, :]
        conv3 = term if conv3 is None else conv3 + term
    sst_ref[0] = cu[t - SC_HIST:, :]
    sext_ref[0:SC_HIST, :] = cu[t - SC_HIST:, :]
    sc_o = sb * conv3

    m_sc = jax.nn.sigmoid(g2) * _dot(sc_o.astype(_BF16), wb_ref[2])
    ffn_piece()
    m_attn = jax.nn.sigmoid(g0) * _dot(attn_o.astype(_BF16), wb_ref[0])
    ffn_piece()
    m_conf = jax.nn.sigmoid(g1) * _dot(conf_o.astype(_BF16), wb_ref[1])
    merged = m_attn + m_conf + m_sc
    x1_new = x + _dot(merged.astype(_BF16), wout_ref[...])
    while ffn["c"] < n_chunks:
        ffn_piece()
    acc = ffn["acc"]
    if final:
        acc = _rmsnorm(acc, fn_ref[...])
    o_ref[...] = acc
    x1_ref[...] = x1_new
    hn_ref[...] = _rmsnorm(x1_new, n2_ref[...]).astype(_BF16)


def _sample_mixer_kernel(*refs, n_carried):
    (sinks_ref, x_ref, cos_ref, sin_ref, ck_ref, cv_ref, sconf_ref, ssc_ref,
     n1_ref, win_ref, dww_ref, dwb_ref, lng_ref, lnb_ref, scw_ref, wb_ref, wout_ref) = refs[:17]
    (xo_ref, ko_ref, vo_ref, co_ref, so_ref,
     q_s, k_s, v_s, u_s, cu_s, attn_s, dw_s, c3_s, cext_s, sext_s) = refs[17 + n_carried:]
    bc = SAMPLE_BATCH_TILE
    ts = x_ref.shape[0] // bc
    x = x_ref[...]
    h = _rmsnorm(x, n1_ref[...]).astype(_BF16)
    cos = cos_ref[...]
    sin = sin_ref[...]
    qkv = _dot(h, win_ref[:, _O_Q:_O_CA])
    cacg = _dot(h, win_ref[:, _O_CA:_O_SB])
    scsx = _dot(h, win_ref[:, _O_SC:_O_GL])
    q_s[...] = _rope(qkv[:, _O_Q:_O_K], cos, sin) * (HEAD_DIM ** -0.5)
    k_s[...] = _rope(qkv[:, _O_K:_O_V], cos, sin)
    v_s[...] = qkv[:, _O_V:_O_CA]
    u_s[...] = cacg[:, :MIX_W] * jax.nn.sigmoid(cacg[:, MIX_W:])
    cu_s[...] = scsx[:, :MIX_W] * scsx[:, MIX_W:]

    cache_len = ck_ref.shape[2]
    tok = lax.broadcasted_iota(jnp.int32, (GROUP * ts, 2 * cache_len), 0) % ts
    lane2 = lax.broadcasted_iota(jnp.int32, (GROUP * ts, 2 * cache_len), 1)
    new_tok = lane2 - (2 * cache_len - ts)
    mask = ((lane2 < ts) & (lane2 > tok)) | ((lane2 >= cache_len) & (new_tok <= tok))
    new_lanes = lax.broadcasted_iota(jnp.int32, (KV_W, cache_len), 1) >= cache_len - ts
    pad_rows = jnp.zeros((cache_len - ts, KV_W), _F32)
    sink_cols = [jnp.concatenate(
        [jnp.full((ts, 1), sinks_ref[kh * GROUP + g], _F32) for g in range(GROUP)], axis=0)
        for kh in range(N_KV_HEADS)]

    def attention(bs_):
        rows = [pl.ds(pl.multiple_of(b * ts, ts), ts) for b in bs_]
        vbs, scores = [], []
        def shifted_in(old_t, new_rows):
            new_t = jnp.concatenate([pad_rows, new_rows], axis=0).T
            return jnp.where(new_lanes, new_t, pltpu.roll(old_t, cache_len - ts, 1))

        for b, rw in zip(bs_, rows):
            k_old = ck_ref[b]
            v_old = cv_ref[b]
            k_out = shifted_in(k_old, k_s[rw, :])
            v_out = shifted_in(v_old, v_s[rw, :])
            ko_ref[b] = k_out
            vo_ref[b] = v_out
            kb = jnp.concatenate([k_old, k_out], axis=1).astype(_BF16)
            vbs.append(jnp.concatenate([v_old, v_out], axis=1).astype(_BF16))
            qrow = q_s[rw, :].astype(_BF16)
            for kh in range(N_KV_HEADS):
                qs = jnp.concatenate(
                    [qrow[:, (kh * GROUP + g) * HEAD_DIM:(kh * GROUP + g + 1) * HEAD_DIM]
                     for g in range(GROUP)], axis=0)
                scores.append(_dot(qs, kb[kh * HEAD_DIM:(kh + 1) * HEAD_DIM, :]))
        probs = []
        for i, s in enumerate(scores):
            sink_col = sink_cols[i % N_KV_HEADS]
            s = jnp.where(mask, s, -jnp.inf)
            m = jnp.maximum(jnp.max(s, axis=-1, keepdims=True), sink_col)
            p = jnp.exp(s - m)
            probs.append((p.astype(_BF16), jnp.sum(p, axis=-1, keepdims=True) + jnp.exp(sink_col - m)))
        for n, rw in enumerate(rows):
            heads = [None] * N_HEADS
            for kh in range(N_KV_HEADS):
                p, denom = probs[n * N_KV_HEADS + kh]
                o = lax.dot_general(p, vbs[n][kh * HEAD_DIM:(kh + 1) * HEAD_DIM, :],
                                    (((1,), (1,)), ((), ())), preferred_element_type=_F32) / denom
                for g in range(GROUP):
                    heads[kh * GROUP + g] = o[g * ts:(g + 1) * ts, :]
            attn_s[rw, :] = jnp.concatenate(heads, axis=-1)

    def convs(b, slot):
        rows = pl.ds(pl.multiple_of(b * ts, ts), ts)
        cext = cext_s.at[slot]
        cext[0:CONF_K - 1, :] = sconf_ref[b]
        cext[CONF_K - 1:CONF_K - 1 + ts, :] = u_s[rows, :]
        dw = None
        for jj in range(CONF_K):
            term = dww_ref[jj:jj + 1, :] * cext[jj:jj + ts, :]
            dw = term if dw is None else dw + term
        dw_s[rows, :] = dw
        co_ref[b] = cext[ts:ts + CONF_K - 1, :]

        sext = sext_s.at[slot]
        sext[0:SC_K - 1, :] = ssc_ref[b]
        sext[SC_K - 1:SC_K - 1 + ts, :] = cu_s[rows, :]
        c3 = None
        for jj in range(SC_K):
            term = scw_ref[jj:jj + 1, :] * sext[jj:jj + ts, :]
            c3 = term if c3 is None else c3 + term
        c3_s[rows, :] = c3
        so_ref[b] = sext[ts:ts + SC_K - 1, :]

    def body(i, carry):
        group = [i * SAMPLE_UNROLL + slot for slot in range(SAMPLE_UNROLL)]
        attention(group)
        for slot, b in enumerate(group):
            convs(b, slot)
        return carry

    lax.fori_loop(0, bc // SAMPLE_UNROLL, body, 0)

    conf_o = jax.nn.silu(_layernorm(dw_s[...] + dwb_ref[...], lng_ref[...], lnb_ref[...]))
    sc_o = _dot(h, win_ref[:, _O_SB:_O_SC]) * c3_s[...]
    xo_ref[...] = x + _branch_merge(h, win_ref, wb_ref, wout_ref, attn_s[...], conf_o, sc_o)


def _ffn_kernel(x_ref, n2_ref, w1_ref, w2_ref, fn_ref, o_ref, *, final):
    x = x_ref[...]
    hn = _rmsnorm(x, n2_ref[...]).astype(_BF16)

    def gate_up(c):
        lo = c * FFN_CHUNK
        return (_dot(hn, w1_ref[:, lo:lo + FFN_CHUNK]),
                _dot(hn, w1_ref[:, D_FF + lo:D_FF + lo + FFN_CHUNK]))

    acc = x
    n_chunks = D_FF // FFN_CHUNK
    g, up = gate_up(0)
    for c in range(n_chunks):
        act = (jax.nn.silu(g) * up).astype(_BF16)
        if c + 1 < n_chunks:
            g, up = gate_up(c + 1)
        acc = acc + _dot(act, w2_ref[c * FFN_CHUNK:(c + 1) * FFN_CHUNK, :])
    if final:
        acc = _rmsnorm(acc, fn_ref[...])
    o_ref[...] = acc


def _resident(shape):
    nd = len(shape)
    return pl.BlockSpec(shape, lambda *_: (0,) * nd, pipeline_mode=pl.Buffered(1))


def _smem():
    return pl.BlockSpec(memory_space=pltpu.SMEM)


def _params(semantics):
    return pltpu.CompilerParams(dimension_semantics=semantics, vmem_limit_bytes=VMEM_LIMIT_BYTES)


def _layer_resident(shape, layer):
    nd = len(shape)
    return pl.BlockSpec((None,) + shape, lambda *_: (layer,) + (0,) * nd, pipeline_mode=pl.Buffered(1))


def _layer_weight_specs(layer):
    return [
        _layer_resident((1, D_MODEL), layer),
        _layer_resident((D_MODEL, IN_W), layer),
        _layer_resident((CONF_K, MIX_W), layer),
        _layer_resident((1, MIX_W), layer),
        _layer_resident((1, MIX_W), layer),
        _layer_resident((1, MIX_W), layer),
        _layer_resident((SC_K, MIX_W), layer),
        _layer_resident((N_BRANCH, MIX_W, D_MODEL), layer),
        _layer_resident((D_MODEL, D_MODEL), layer),
    ]


def _prompt_layer(layer, x, cos, sin, sinks, lw, n2, w1, w2, fn, final):
    n_tok = x.shape[0]
    s = cos.shape[0]
    b = n_tok // s
    t = PROMPT_TILE
    tps = s // t
    n_tiles = n_tok // t
    cur = lambda i: jnp.minimum(i, n_tiles - 1)
    prev = lambda i: jnp.maximum(i - 1, 0)
    state = lambda i: (cur(i) // tps, 0, 0)
    return pl.pallas_call(
        functools.partial(_prompt_layer_kernel, tiles_per_seq=tps, n_tiles=n_tiles, final=final),
        grid=(n_tiles + 1,),
        in_specs=[_smem(),
                  pl.BlockSpec((t, D_MODEL), lambda i: (cur(i), 0)),
                  pl.BlockSpec((t, LANES), lambda i: (cur(i) % tps, 0)),
                  pl.BlockSpec((t, LANES), lambda i: (cur(i) % tps, 0))] + _layer_weight_specs(layer)
                 + [_layer_resident((1, D_MODEL), layer),
                    _layer_resident((D_MODEL, 2 * D_FF), layer),
                    _layer_resident((D_FF, D_MODEL), layer),
                    _resident((1, D_MODEL))],
        out_specs=[pl.BlockSpec((t, D_MODEL), lambda i: (prev(i), 0)),
                   pl.BlockSpec((1, WINDOW, KV_W), state),
                   pl.BlockSpec((1, WINDOW, KV_W), state),
                   pl.BlockSpec((1, CONF_HIST, MIX_W), state),
                   pl.BlockSpec((1, SC_HIST, MIX_W), state)],
        out_shape=[jax.ShapeDtypeStruct((n_tok, D_MODEL), _F32),
                   jax.ShapeDtypeStruct((b, WINDOW, KV_W), _F32),
                   jax.ShapeDtypeStruct((b, WINDOW, KV_W), _F32),
                   jax.ShapeDtypeStruct((b, CONF_HIST, MIX_W), _F32),
                   jax.ShapeDtypeStruct((b, SC_HIST, MIX_W), _F32)],
        scratch_shapes=[pltpu.VMEM((t, D_MODEL), _F32),
                        pltpu.VMEM((t, D_MODEL), _BF16),
                        pltpu.VMEM((BLOCK, KV_W), _BF16),
                        pltpu.VMEM((BLOCK, KV_W), _BF16),
                        pltpu.VMEM((CONF_HIST + t + SUBLANES, MIX_W), _F32),
                        pltpu.VMEM((SC_HIST + t, MIX_W), _F32)],
        compiler_params=_params(("arbitrary",)),
        name="prompt_layer",
    )(sinks, x, cos, sin, *lw, n2, w1, w2, fn)


def _sample_mixer(layer, x, cos, sin, ck, cv, sconf, ssc, sinks, lw, carried):
    n_tok = x.shape[0]
    bc = SAMPLE_BATCH_TILE
    nb, cache_len = ck.shape[1], ck.shape[3]
    ts = n_tok // nb
    r = bc * ts
    rows = lambda i: (i, 0)
    seqs = lambda i: (layer, i, 0, 0)
    state_specs = [pl.BlockSpec((None, bc, KV_W, cache_len), seqs),
                   pl.BlockSpec((None, bc, KV_W, cache_len), seqs),
                   pl.BlockSpec((None, bc, CONF_K - 1, MIX_W), seqs),
                   pl.BlockSpec((None, bc, SC_K - 1, MIX_W), seqs)]
    n_fixed_in = 8 + len(lw)
    return pl.pallas_call(
        functools.partial(_sample_mixer_kernel, n_carried=len(carried)),
        grid=(nb // bc,),
        in_specs=[_smem(),
                  pl.BlockSpec((r, D_MODEL), rows),
                  _resident((r, LANES)),
                  _resident((r, LANES))] + state_specs + _layer_weight_specs(layer)
                 + [pl.BlockSpec(memory_space=pl.ANY)] * len(carried),
        out_specs=[pl.BlockSpec((r, D_MODEL), rows)] + state_specs,
        out_shape=[jax.ShapeDtypeStruct((n_tok, D_MODEL), _F32),
                   jax.ShapeDtypeStruct(ck.shape, _F32),
                   jax.ShapeDtypeStruct(cv.shape, _F32),
                   jax.ShapeDtypeStruct(sconf.shape, _F32),
                   jax.ShapeDtypeStruct(ssc.shape, _F32)],
        input_output_aliases={n_fixed_in + i: 1 + i for i in range(len(carried))},
        scratch_shapes=[pltpu.VMEM((r, Q_W), _F32),
                        pltpu.VMEM((r, KV_W), _F32),
                        pltpu.VMEM((r, KV_W), _F32),
                        pltpu.VMEM((r, MIX_W), _F32),
                        pltpu.VMEM((r, MIX_W), _F32),
                        pltpu.VMEM((r, Q_W), _F32),
                        pltpu.VMEM((r, MIX_W), _F32),
                        pltpu.VMEM((r, MIX_W), _F32),
                        pltpu.VMEM((SAMPLE_UNROLL, CONF_K - 1 + ts, MIX_W), _F32),
                        pltpu.VMEM((SAMPLE_UNROLL, SC_K - 1 + ts, MIX_W), _F32)],
        compiler_params=_params(("arbitrary",)),
        name="sample_mixer",
    )(sinks, x, cos, sin, ck, cv, sconf, ssc, *lw, *carried)


def _ffn(layer, x, n2, w1, w2, fn, final, name):
    n_tok = x.shape[0]
    t = FFN_TILE
    rows = lambda i: (i, 0)
    return pl.pallas_call(
        functools.partial(_ffn_kernel, final=final),
        grid=(n_tok // t,),
        in_specs=[pl.BlockSpec((t, D_MODEL), rows),
                  _layer_resident((1, D_MODEL), layer),
                  _layer_resident((D_MODEL, 2 * D_FF), layer),
                  _layer_resident((D_FF, D_MODEL), layer),
                  _resident((1, D_MODEL))],
        out_specs=pl.BlockSpec((t, D_MODEL), rows),
        out_shape=jax.ShapeDtypeStruct((n_tok, D_MODEL), _F32),
        compiler_params=_params(("arbitrary",)),
        name=name,
    )(x, n2, w1, w2, fn)


def _rope_tables(positions):
    half = HEAD_DIM // 2
    inv = ROPE_THETA ** (-np.arange(half, dtype=np.float64) / half)
    ang = np.asarray(positions, np.float64)[:, None] * inv[None, :]
    cos = np.cos(ang)
    sin = np.sin(ang)
    cos_t = np.tile(np.concatenate([cos, cos], axis=-1), (1, LANES // HEAD_DIM))
    sin_t = np.tile(np.concatenate([-sin, sin], axis=-1), (1, LANES // HEAD_DIM))
    return jnp.asarray(cos_t, _F32), jnp.asarray(sin_t, _F32)


def kernel(x_prompt, x_sample, cache_k, cache_v, state_conf, state_sconv, norm1, w_in, sinks,
           conf_dw_w, conf_dw_b, conf_ln_g, conf_ln_b, sconv_w, w_branch, w_out, norm2,
           w_ffn_in, w_ffn_out, final_norm):
    depth = w_in.shape[0]
    bp, sp, _ = x_prompt.shape
    bs, ts, _ = x_sample.shape
    cache_len = cache_k.shape[2]
    assert sp % PROMPT_TILE == 0 and PROMPT_TILE % BLOCK == 0 and sp >= WINDOW
    assert bs % SAMPLE_BATCH_TILE == 0 and SAMPLE_BATCH_TILE % SAMPLE_UNROLL == 0
    assert ts % SUBLANES == 0 and cache_len == WINDOW
    assert (bs * ts) % FFN_TILE == 0 and D_FF % FFN_CHUNK == 0

    cos_p, sin_p = _rope_tables(np.arange(sp))
    cos_s, sin_s = _rope_tables(np.tile(PAST_LEN + np.arange(ts), SAMPLE_BATCH_TILE))

    xp = x_prompt.reshape(bp * sp, D_MODEL)
    xs = x_sample.reshape(bs * ts, D_MODEL)
    ck = cache_k.transpose(0, 1, 3, 4, 2).reshape(depth, bs, KV_W, cache_len)
    cv = cache_v.transpose(0, 1, 3, 4, 2).reshape(depth, bs, KV_W, cache_len)
    fn = final_norm.reshape(1, D_MODEL)
    prompt_states = [[] for _ in range(4)]
    sample_states = tuple(jnp.zeros(a.shape, _F32) for a in (ck, cv, state_conf, state_sconv))
    lw = (norm1.reshape(depth, 1, D_MODEL), w_in.astype(_BF16), conf_dw_w,
          conf_dw_b.reshape(depth, 1, MIX_W), conf_ln_g.reshape(depth, 1, MIX_W),
          conf_ln_b.reshape(depth, 1, MIX_W), sconv_w, w_branch.astype(_BF16), w_out.astype(_BF16))
    n2 = norm2.reshape(depth, 1, D_MODEL)
    w1 = w_ffn_in.astype(_BF16)
    w2 = w_ffn_out.astype(_BF16)
    for l in range(depth):
        final = l == depth - 1
        xp, kp, vp, cp, scp = _prompt_layer(l, xp, cos_p, sin_p, sinks[l], lw, n2, w1, w2, fn, final)
        xs, *sample_states = _sample_mixer(l, xs, cos_s, sin_s, ck, cv, state_conf, state_sconv,
                                           sinks[l], lw, tuple(sample_states))
        xs = _ffn(l, xs, n2, w1, w2, fn, final, "sample_ffn")

        prompt_states[0].append(kp.reshape(bp, WINDOW, N_KV_HEADS, HEAD_DIM))
        prompt_states[1].append(vp.reshape(bp, WINDOW, N_KV_HEADS, HEAD_DIM))
        prompt_states[2].append(cp[:, CONF_HIST - (CONF_K - 1):, :])
        prompt_states[3].append(scp[:, SC_HIST - (SC_K - 1):, :])
    k_s, v_s, c_s, s_s = sample_states
    kv_t_shape = (depth, bs, N_KV_HEADS, HEAD_DIM, cache_len)
    k_s = k_s.reshape(kv_t_shape).transpose(0, 1, 4, 2, 3)
    v_s = v_s.reshape(kv_t_shape).transpose(0, 1, 4, 2, 3)
    return ((xp.reshape(bp, sp, D_MODEL), xs.reshape(bs, ts, D_MODEL)) + tuple(jnp.stack(o) for o in prompt_states)
            + (k_s, v_s, c_s, s_s))
```
